```python
import jax, jax.numpy as jnp
from jax import lax
import numpy as np

D_MODEL = 1024
BATCH = 4
SEQ = 8192
DEPTH = 1
DEC_BATCH = 128
DEC_SEQ = 1
PAST_LEN = 16384
PAGE_SIZE = 128

N_HEADS = 8
Q_LORA = 256
KV_LORA = 128
QK_NOPE = 64
QK_ROPE = 32
V_HEAD = 64
MLA_WIDTH = N_HEADS * V_HEAD
ROPE_THETA = 10000.0
MLA_SCALE = (QK_NOPE + QK_ROPE) ** -0.5
CONV_WIDTH = 256
CONV_K = 3
N_MEM = 256
MEM_HEADS = 4
MEM_HEAD_DIM = 64
MEM_WIDTH = MEM_HEADS * MEM_HEAD_DIM
MEM_SCALE = MEM_HEAD_DIM ** -0.5
D_MIX = MLA_WIDTH + CONV_WIDTH + MEM_WIDTH
D_FF = 2816
EPS = 1e-6
Q_BLOCK = 128
IN_SPLIT_SIZES = (Q_LORA, KV_LORA, QK_ROPE, CONV_WIDTH, CONV_WIDTH, CONV_WIDTH, MEM_WIDTH)
IN_WIDTH = Q_LORA + KV_LORA + QK_ROPE + 3 * CONV_WIDTH + MEM_WIDTH

kernel_name = "hymba_mla_shortconv_memory_macaron_step"


def rmsnorm(x, g):
    xf = x.astype(jnp.float32)
    inv = lax.rsqrt(jnp.mean(xf * xf, axis=-1, keepdims=True) + EPS)
    return (xf * inv * g.astype(jnp.float32)).astype(x.dtype)


def swiglu(x, w_gate, w_up, w_down):
    return (jax.nn.silu(x @ w_gate) * (x @ w_up)) @ w_down


def rope_cos_sin(pos):
    inv_freq = ROPE_THETA ** (-jnp.arange(0, QK_ROPE, 2, dtype=jnp.float32) / QK_ROPE)
    ang = pos.astype(jnp.float32)[:, None] * inv_freq[None, :]
    return jnp.cos(ang), jnp.sin(ang)


def apply_rope(x, cos, sin):
    xf = x.astype(jnp.float32)
    x1, x2 = jnp.split(xf, 2, axis=-1)
    return jnp.concatenate([x1 * cos - x2 * sin, x1 * sin + x2 * cos], axis=-1).astype(x.dtype)


def pre_mix(x, cos, sin, g_ffn1, w1_gate, w1_up, w1_down, g_mix, w_in, g_q_lora, w_uq, g_qn, g_qr,
            g_kv_lora, g_kr, g_mq):
    x = x + 0.5 * swiglu(rmsnorm(x, g_ffn1), w1_gate, w1_up, w1_down)
    h = rmsnorm(x, g_mix)
    z = h @ w_in
    split_at = np.cumsum(IN_SPLIT_SIZES)[:-1].tolist()
    c_q, c_kv, k_r, u_in, gate_b, gate_c, m_q = jnp.split(z, split_at, axis=-1)
    B, T = x.shape[0], x.shape[1]
    q = (rmsnorm(c_q, g_q_lora) @ w_uq).reshape(B, T, N_HEADS, QK_NOPE + QK_ROPE)
    q_nope = rmsnorm(q[..., :QK_NOPE], g_qn)
    q_rope = apply_rope(rmsnorm(q[..., QK_NOPE:], g_qr), cos[:, None, :], sin[:, None, :])
    c_kv = rmsnorm(c_kv, g_kv_lora)
    k_rope = apply_rope(rmsnorm(k_r, g_kr), cos, sin)
    u = gate_c * u_in
    m_q = rmsnorm(m_q.reshape(B, T, MEM_HEADS, MEM_HEAD_DIM), g_mq)
    return x, q_nope, q_rope, c_kv, k_rope, u, gate_b, m_q


def latent_to_kv(c_kv, w_uk, w_uv, g_kn):
    k_nope = rmsnorm(jnp.einsum("bkl,lhd->bkhd", c_kv, w_uk), g_kn)
    v = jnp.einsum("bkl,lhd->bkhd", c_kv, w_uv)
    return k_nope, v


def mla_scores(q_nope, q_rope, k_nope, k_rope):
    s = jnp.einsum("bthd,bkhd->bhtk", q_nope, k_nope, preferred_element_type=jnp.float32)
    s = s + jnp.einsum("bthr,bkr->bhtk", q_rope, k_rope, preferred_element_type=jnp.float32)
    return s * MLA_SCALE


def mla_prompt(q_nope, q_rope, c_kv, k_rope, w_uk, w_uv, g_kn):
    B, S = q_nope.shape[0], q_nope.shape[1]
    k_nope, v = latent_to_kv(c_kv, w_uk, w_uv, g_kn)
    n_blk = S // Q_BLOCK
    qn_b = q_nope.reshape(B, n_blk, Q_BLOCK, N_HEADS, QK_NOPE).transpose(1, 0, 2, 3, 4)
    qr_b = q_rope.reshape(B, n_blk, Q_BLOCK, N_HEADS, QK_ROPE).transpose(1, 0, 2, 3, 4)
    key_pos = jnp.arange(S)

    def block(args):
        i, qn, qr = args
        s = mla_scores(qn, qr, k_nope, k_rope)
        q_pos = i * Q_BLOCK + jnp.arange(Q_BLOCK)
        mask = key_pos[None, :] <= q_pos[:, None]
        p = jax.nn.softmax(jnp.where(mask, s, -jnp.inf), axis=-1).astype(v.dtype)
        return jnp.einsum("bhqk,bkhd->bqhd", p, v)

    o = lax.map(block, (jnp.arange(n_blk), qn_b, qr_b))
    return o.transpose(1, 0, 2, 3, 4).reshape(B, S, MLA_WIDTH)


def mla_sample(q_nope, q_rope, c_kv_new, k_rope_new, cache_ckv, cache_krope, page_table, w_uk, w_uv, g_kn):
    B, T = q_nope.shape[0], q_nope.shape[1]

    def merge(carry, s, v):
        m, l, acc = carry
        m_new = jnp.maximum(m, jnp.max(s, axis=-1))
        corr = jnp.exp(m - m_new)
        p = jnp.exp(s - m_new[..., None])
        l_new = l * corr + jnp.sum(p, axis=-1)
        acc_new = acc * corr[..., None] + jnp.einsum("bhtk,bkhd->bhtd", p, v.astype(jnp.float32))
        return (m_new, l_new, acc_new)

    def page_step(carry, page_ids):
        ckv = cache_ckv[page_ids]
        kr = cache_krope[page_ids]
        k_nope, v = latent_to_kv(ckv, w_uk, w_uv, g_kn)
        s = mla_scores(q_nope, q_rope, k_nope, kr)
        return merge(carry, s, v), None

    init = (jnp.full((B, N_HEADS, T), -jnp.inf, jnp.float32),
            jnp.zeros((B, N_HEADS, T), jnp.float32),
            jnp.zeros((B, N_HEADS, T, V_HEAD), jnp.float32))
    carry, _ = lax.scan(page_step, init, page_table.T)
    k_nope, v = latent_to_kv(c_kv_new, w_uk, w_uv, g_kn)
    s = mla_scores(q_nope, q_rope, k_nope, k_rope_new)
    causal = jnp.arange(T)[None, :] <= jnp.arange(T)[:, None]
    m, l, acc = merge(carry, jnp.where(causal, s, -jnp.inf), v)
    o = (acc / l[..., None]).astype(q_nope.dtype)
    return o.transpose(0, 2, 1, 3).reshape(B, T, MLA_WIDTH)


def short_conv(u_ext, conv_w):
    T = u_ext.shape[1] - (CONV_K - 1)
    y = conv_w[0] * u_ext[:, 0:T]
    for k in range(1, CONV_K):
        y = y + conv_w[k] * u_ext[:, k:k + T]
    return y


def mem_kv(mem, g_mem, w_mem_k, w_mem_v, g_mk):
    B, N = mem.shape[0], mem.shape[1]
    hm = rmsnorm(mem, g_mem)
    k = rmsnorm((hm @ w_mem_k).reshape(B, N, MEM_HEADS, MEM_HEAD_DIM), g_mk)
    v = (hm @ w_mem_v).reshape(B, N, MEM_HEADS, MEM_HEAD_DIM)
    return k, v


def mem_attend(m_q, mem_k, mem_v):
    B, T = m_q.shape[0], m_q.shape[1]
    s = jnp.einsum("bthd,bnhd->bhtn", m_q, mem_k, preferred_element_type=jnp.float32) * MEM_SCALE
    p = jax.nn.softmax(s, axis=-1).astype(mem_v.dtype)
    return jnp.einsum("bhtn,bnhd->bthd", p, mem_v).reshape(B, T, MEM_WIDTH)


def post_mix(x, o_mla, y_conv, o_mem, g_out_mla, g_out_conv, g_out_mem, w_o, g_ffn2, w2_gate, w2_up, w2_down):
    o = jnp.concatenate([rmsnorm(o_mla, g_out_mla), rmsnorm(y_conv, g_out_conv), rmsnorm(o_mem, g_out_mem)], axis=-1)
    x = x + o @ w_o
    return x + 0.5 * swiglu(rmsnorm(x, g_ffn2), w2_gate, w2_up, w2_down)


def setup_inputs(seed: int = 0) -> dict:
    key = jax.random.key(seed)
    ks = iter(jax.random.split(key, 48))

    def nrm(shape, scale=1.0):
        return jax.random.normal(next(ks), shape, jnp.float32) * scale

    def gain(width):
        return 1.0 + nrm((DEPTH, width), 0.05)

    n_pages = PAST_LEN // PAGE_SIZE
    n_used = DEC_BATCH * n_pages
    n_phys = n_used + n_used // 4
    page_table = jax.random.permutation(next(ks), n_phys)[:n_used].reshape(DEC_BATCH, n_pages).astype(jnp.int32)
    return {
        "x_prompt": nrm((BATCH, SEQ, D_MODEL)),
        "mem_prompt": nrm((BATCH, N_MEM, D_MODEL)),
        "x_sample": nrm((DEC_BATCH, DEC_SEQ, D_MODEL)),
        "cache_ckv": nrm((DEPTH, n_phys, PAGE_SIZE, KV_LORA)),
        "cache_krope": nrm((DEPTH, n_phys, PAGE_SIZE, QK_ROPE)),
        "page_table": page_table,
        "state_conv": nrm((DEPTH, DEC_BATCH, CONV_K - 1, CONV_WIDTH)),
        "cache_mem_k": nrm((DEPTH, DEC_BATCH, N_MEM, MEM_HEADS, MEM_HEAD_DIM)),
        "cache_mem_v": nrm((DEPTH, DEC_BATCH, N_MEM, MEM_HEADS, MEM_HEAD_DIM)),
        "g_ffn1": gain(D_MODEL),
        "w1_gate": nrm((DEPTH, D_MODEL, D_FF), D_MODEL ** -0.5),
        "w1_up": nrm((DEPTH, D_MODEL, D_FF), D_MODEL ** -0.5),
        "w1_down": nrm((DEPTH, D_FF, D_MODEL), D_FF ** -0.5),
        "g_mix": gain(D_MODEL),
        "w_in": nrm((DEPTH, D_MODEL, IN_WIDTH), D_MODEL ** -0.5),
        "g_q_lora": gain(Q_LORA),
        "w_uq": nrm((DEPTH, Q_LORA, N_HEADS * (QK_NOPE + QK_ROPE)), Q_LORA ** -0.5),
        "g_qn": gain(QK_NOPE),
        "g_qr": gain(QK_ROPE),
        "g_kv_lora": gain(KV_LORA),
        "w_uk": nrm((DEPTH, KV_LORA, N_HEADS, QK_NOPE), KV_LORA ** -0.5),
        "w_uv": nrm((DEPTH, KV_LORA, N_HEADS, V_HEAD), KV_LORA ** -0.5),
        "g_kn": gain(QK_NOPE),
        "g_kr": gain(QK_ROPE),
        "conv_w": nrm((DEPTH, CONV_K, CONV_WIDTH), CONV_K ** -0.5),
        "g_mem": gain(D_MODEL),
        "w_mem_k": nrm((DEPTH, D_MODEL, MEM_WIDTH), D_MODEL ** -0.5),
        "w_mem_v": nrm((DEPTH, D_MODEL, MEM_WIDTH), D_MODEL ** -0.5),
        "g_mq": gain(MEM_HEAD_DIM),
        "g_mk": gain(MEM_HEAD_DIM),
        "g_out_mla": gain(MLA_WIDTH),
        "g_out_conv": gain(CONV_WIDTH),
        "g_out_mem": gain(MEM_WIDTH),
        "w_o": nrm((DEPTH, D_MIX, D_MODEL), D_MIX ** -0.5),
        "g_ffn2": gain(D_MODEL),
        "w2_gate": nrm((DEPTH, D_MODEL, D_FF), D_MODEL ** -0.5),
        "w2_up": nrm((DEPTH, D_MODEL, D_FF), D_MODEL ** -0.5),
        "w2_down": nrm((DEPTH, D_FF, D_MODEL), D_FF ** -0.5),
    }


def reference(x_prompt, mem_prompt, x_sample, cache_ckv, cache_krope, page_table, state_conv, cache_mem_k,
              cache_mem_v, g_ffn1, w1_gate, w1_up, w1_down, g_mix, w_in, g_q_lora, w_uq, g_qn, g_qr, g_kv_lora,
              w_uk, w_uv, g_kn, g_kr, conv_w, g_mem, w_mem_k, w_mem_v, g_mq, g_mk, g_out_mla, g_out_conv,
              g_out_mem, w_o, g_ffn2, w2_gate, w2_up, w2_down):
    cos_p, sin_p = rope_cos_sin(jnp.arange(SEQ))
    cos_s, sin_s = rope_cos_sin(PAST_LEN + jnp.arange(DEC_SEQ))
    xp, xs = x_prompt, x_sample
    ckv_p_l, kr_p_l, conv_p_l, mk_p_l, mv_p_l = [], [], [], [], []
    ckv_s_l, kr_s_l, conv_s_l = [], [], []
    for l in range(DEPTH):
        pre_w = (g_ffn1[l], w1_gate[l], w1_up[l], w1_down[l], g_mix[l], w_in[l], g_q_lora[l], w_uq[l],
                 g_qn[l], g_qr[l], g_kv_lora[l], g_kr[l], g_mq[l])
        post_w = (g_out_mla[l], g_out_conv[l], g_out_mem[l], w_o[l], g_ffn2[l], w2_gate[l], w2_up[l], w2_down[l])

        xp, qn, qr, ckv, kr, u, gb, mq = pre_mix(xp, cos_p, sin_p, *pre_w)
        o_mla = mla_prompt(qn, qr, ckv, kr, w_uk[l], w_uv[l], g_kn[l])
        u_ext = jnp.pad(u, ((0, 0), (CONV_K - 1, 0), (0, 0)))
        y_conv = gb * short_conv(u_ext, conv_w[l])
        mk, mv = mem_kv(mem_prompt, g_mem[l], w_mem_k[l], w_mem_v[l], g_mk[l])
        o_mem = mem_attend(mq, mk, mv)
        xp = post_mix(xp, o_mla, y_conv, o_mem, *post_w)
        ckv_p_l.append(ckv)
        kr_p_l.append(kr)
        conv_p_l.append(u[:, -(CONV_K - 1):])
        mk_p_l.append(mk)
        mv_p_l.append(mv)

        xs, qn, qr, ckv, kr, u, gb, mq = pre_mix(xs, cos_s, sin_s, *pre_w)
        o_mla = mla_sample(qn, qr, ckv, kr, cache_ckv[l], cache_krope[l], page_table, w_uk[l], w_uv[l], g_kn[l])
        u_ext = jnp.concatenate([state_conv[l].astype(u.dtype), u], axis=1)
        y_conv = gb * short_conv(u_ext, conv_w[l])
        o_mem = mem_attend(mq, cache_mem_k[l], cache_mem_v[l])
        xs = post_mix(xs, o_mla, y_conv, o_mem, *post_w)
        ckv_s_l.append(ckv)
        kr_s_l.append(kr)
        conv_s_l.append(u_ext[:, -(CONV_K - 1):])

    return (xp, xs, jnp.stack(ckv_p_l), jnp.stack(kr_p_l), jnp.stack(conv_p_l), jnp.stack(mk_p_l),
            jnp.stack(mv_p_l), jnp.stack(ckv_s_l), jnp.stack(kr_s_l), jnp.stack(conv_s_l))
```

```python
import functools

import jax
import jax.numpy as jnp
from jax import lax
from jax.experimental import pallas as pl
from jax.experimental.pallas import tpu as pltpu

F32 = jnp.float32
BF16 = jnp.bfloat16

D_MODEL = 1024
N_HEADS = 8
Q_LORA = 256
KV_LORA = 128
QK_NOPE = 64
QK_ROPE = 32
V_HEAD = 64
MLA_WIDTH = N_HEADS * V_HEAD
CONV_WIDTH = 256
CONV_K = 3
MEM_HEADS = 4
MEM_HEAD_DIM = 64
MEM_WIDTH = MEM_HEADS * MEM_HEAD_DIM
D_FF = 2816
ROPE_THETA = 10000.0
EPS = 1e-6
PAST_LEN = 16384
PAGE_SIZE = 128
MLA_SCALE = (QK_NOPE + QK_ROPE) ** -0.5
MEM_SCALE = MEM_HEAD_DIM ** -0.5
LOG2E = 1.4426950408889634

HEAD_SLOT = 128
ROPE_LO = QK_NOPE
HALF_ROPE = QK_ROPE // 2
Z_WIDTH = 1536
VT_ROWS = 80
NEG = -1e30
PAGES_PER_CHUNK = 4


def _rms(x, g):
    ms = jnp.mean(x * x, axis=-1, keepdims=True)
    return x * lax.rsqrt(ms + EPS) * g


def _dot(a, b):
    return jnp.dot(a, b, preferred_element_type=F32)


def _dot_nt(a, b):
    return lax.dot_general(a, b, (((1,), (1,)), ((), ())), preferred_element_type=F32)


def _group_mean_sq(x, g_ref):
    x2 = x * x
    hi = x2.astype(BF16)
    lo = (x2 - hi.astype(F32)).astype(BF16)
    g = g_ref[...]
    cols = []
    for j in range(x.shape[1] // 256):
        sl = slice(j * 256, (j + 1) * 256)
        cols.append(_dot(hi[:, sl], g) + _dot(lo[:, sl], g))
    return cols[0] if len(cols) == 1 else jnp.concatenate(cols, axis=1)


def _rope_slab(v, cos, s1, s2):
    return (v * cos + pltpu.roll(v, HEAD_SLOT - HALF_ROPE, 1) * s1
            + pltpu.roll(v, HALF_ROPE, 1) * s2)


def _const_spec(shape):
    nd = len(shape)
    return pl.BlockSpec(shape, lambda *_: (0,) * nd, pipeline_mode=pl.Buffered(1))


def _ffn_body(x_ref, g_ref, wg_ref, wu_ref, wd_ref, o_ref):
    x = x_ref[...]
    xn = _rms(x, g_ref[...]).astype(BF16)
    h = _dot(xn, wg_ref[...])
    u = _dot(xn, wu_ref[...])
    a = (h / (1.0 + jnp.exp(-h)) * u).astype(BF16)
    o_ref[...] = x + 0.5 * _dot(a, wd_ref[...])


def _ffn_half(x, g, wg, wu, wd, tm):
    n, d = x.shape
    dff = wg.shape[1]
    return pl.pallas_call(
        _ffn_body,
        grid=(n // tm,),
        in_specs=[pl.BlockSpec((tm, d), lambda i: (i, 0)),
                  _const_spec((1, d)), _const_spec((d, dff)), _const_spec((d, dff)),
                  _const_spec((dff, d))],
        out_specs=pl.BlockSpec((tm, d), lambda i: (i, 0)),
        out_shape=jax.ShapeDtypeStruct((n, d), F32),
        compiler_params=pltpu.CompilerParams(dimension_semantics=("parallel",)),
        name="ffn_half",
    )(x, g, wg, wu, wd)


def _premix_body(x_ref, cos_ref, s1_ref, s2_ref, gmix_ref, win_ref, gql_ref, wuq_ref, gq_ref,
                 g2_ref, gkvl_ref, gkr_ref, wuk_ref, gk_ref, wuv_ref, gmq_ref, g64_ref,
                 qt_ref, k_ref, vt_ref, ckv_ref, kr_ref, u_ref, gb_ref, mq_ref):
    tm = x_ref.shape[0]
    hn = _rms(x_ref[...], gmix_ref[...]).astype(BF16)
    z = _dot(hn, win_ref[...])
    c_q, c_kv, k_r = z[:, 0:256], z[:, 256:384], z[:, 384:512]
    u_in, g_b, g_c, m_q = z[:, 512:768], z[:, 768:1024], z[:, 1024:1280], z[:, 1280:1536]
    cos, s1, s2 = cos_ref[...], s1_ref[...], s2_ref[...]

    q = _dot(_rms(c_q, gql_ref[...]).astype(BF16), wuq_ref[...])
    q = q * lax.rsqrt(_group_mean_sq(q, g2_ref) + EPS) * gq_ref[...]
    q = jnp.concatenate(
        [_rope_slab(q[:, h * HEAD_SLOT:(h + 1) * HEAD_SLOT], cos, s1, s2) for h in range(N_HEADS)],
        axis=1)
    qt_ref[...] = q.T.astype(BF16)

    ckv = _rms(c_kv, gkvl_ref[...])
    ckv_ref[...] = ckv
    ckv16 = ckv.astype(BF16)
    kr_ms = jnp.sum(k_r * k_r, axis=-1, keepdims=True) * (1.0 / QK_ROPE)
    krr = _rope_slab(k_r * lax.rsqrt(kr_ms + EPS) * gkr_ref[...], cos, s1, s2)
    kr_ref[...] = krr[:, ROPE_LO:ROPE_LO + QK_ROPE]
    kk = _dot(ckv16, wuk_ref[...])
    kk = kk * lax.rsqrt(_group_mean_sq(kk, g2_ref) + EPS) * gk_ref[...]
    k_ref[...] = jnp.concatenate(
        [kk[:, h * HEAD_SLOT:(h + 1) * HEAD_SLOT] + krr for h in range(N_HEADS)],
        axis=1).astype(BF16)
    vt = _dot(ckv16, wuv_ref[...]).T
    ones_rows = jnp.where(lax.broadcasted_iota(jnp.int32, (VT_ROWS - V_HEAD, tm), 0) == 0,
                          1.0, 0.0).astype(BF16)
    for h in range(N_HEADS):
        vt_ref[h * VT_ROWS:h * VT_ROWS + V_HEAD, :] = vt[h * V_HEAD:(h + 1) * V_HEAD].astype(BF16)
        vt_ref[h * VT_ROWS + V_HEAD:(h + 1) * VT_ROWS, :] = ones_rows

    u_ref[...] = g_c * u_in
    gb_ref[...] = g_b
    mq = m_q * lax.rsqrt(_group_mean_sq(m_q, g64_ref) + EPS) * gmq_ref[...]
    mq_ref[...] = mq.astype(BF16)


def _premix(x, tables, w, tm):
    b, s, d = x.shape
    cos, s1, s2 = tables
    row = lambda width: pl.BlockSpec((None, tm, width), lambda bi, i: (bi, i, 0))
    tab = pl.BlockSpec((tm, HEAD_SLOT), lambda bi, i: (i, 0))
    consts = [w["g_mix"], w["w_in"], w["g_q_lora"], w["w_uq"], w["gq_vec"], w["G2"], w["g_kv_lora"],
              w["gkr_vec"], w["w_uk"], w["gk_vec"], w["w_uv"], w["gmq_vec"], w["G64"]]
    out_shape = (
        jax.ShapeDtypeStruct((b, N_HEADS * HEAD_SLOT, s), BF16),
        jax.ShapeDtypeStruct((b, s, N_HEADS * HEAD_SLOT), BF16),
        jax.ShapeDtypeStruct((b, s // tm, N_HEADS * VT_ROWS, tm), BF16),
        jax.ShapeDtypeStruct((b, s, KV_LORA), F32),
        jax.ShapeDtypeStruct((b, s, QK_ROPE), F32),
        jax.ShapeDtypeStruct((b, s, CONV_WIDTH), F32),
        jax.ShapeDtypeStruct((b, s, CONV_WIDTH), F32),
        jax.ShapeDtypeStruct((b, s, MEM_WIDTH), BF16),
    )
    out_specs = (
        pl.BlockSpec((None, N_HEADS * HEAD_SLOT, tm), lambda bi, i: (bi, 0, i)),
        row(N_HEADS * HEAD_SLOT),
        pl.BlockSpec((None, None, N_HEADS * VT_ROWS, tm), lambda bi, i: (bi, i, 0, 0)),
        row(KV_LORA), row(QK_ROPE), row(CONV_WIDTH), row(CONV_WIDTH), row(MEM_WIDTH),
    )
    return pl.pallas_call(
        _premix_body,
        grid=(b, s // tm),
        in_specs=[row(d), tab, tab, tab] + [_const_spec(c.shape) for c in consts],
        out_specs=out_specs,
        out_shape=out_shape,
        compiler_params=pltpu.CompilerParams(dimension_semantics=("parallel", "parallel")),
        name="premix",
    )(x, cos, s1, s2, *consts)


def _attn_body(qt_ref, k_ref, vt_ref, o_ref):
    tq = qt_ref.shape[1]
    tk = vt_ref.shape[2]
    qi = pl.program_id(2)
    kpos = lax.broadcasted_iota(jnp.int32, (tk, tq), 0)
    qpos = lax.broadcasted_iota(jnp.int32, (tk, tq), 1)
    outs = []
    for j in range(2):
        q_t = qt_ref[j * HEAD_SLOT:(j + 1) * HEAD_SLOT, :]

        def tile(kj, m, acc, masked, j=j, q_t=q_t):
            k = k_ref[pl.ds(pl.multiple_of(kj * tk, tk), tk), j * HEAD_SLOT:(j + 1) * HEAD_SLOT]
            s_t = _dot(k, q_t)
            if masked:
                s_t = jnp.where(kpos <= qpos, s_t, NEG)
            m_new = jnp.maximum(m, jnp.max(s_t, axis=0, keepdims=True))
            alpha = jnp.exp2(m - m_new)
            p = jnp.exp2(s_t - m_new).astype(BF16)
            v = vt_ref[kj, j * VT_ROWS:(j + 1) * VT_ROWS, :]
            return m_new, alpha * acc + _dot(v, p)

        m0 = jnp.full((1, tq), NEG, F32)
        acc0 = jnp.zeros((VT_ROWS, tq), F32)
        m, acc = lax.fori_loop(0, qi, lambda kj, c: tile(kj, c[0], c[1], False), (m0, acc0))
        m, acc = tile(qi, m, acc, True)
        outs.append(acc[0:V_HEAD] / acc[V_HEAD:V_HEAD + 1])
    o_ref[...] = jnp.concatenate(outs, axis=0).T


def _mla_prompt(qt, k, vt):
    b, _, s = qt.shape
    nkt, tk = vt.shape[1], vt.shape[3]
    tq = tk
    return pl.pallas_call(
        _attn_body,
        grid=(b, N_HEADS // 2, s // tq),
        in_specs=[pl.BlockSpec((None, 2 * HEAD_SLOT, tq), lambda bi, p, qi: (bi, p, qi)),
                  pl.BlockSpec((None, s, 2 * HEAD_SLOT), lambda bi, p, qi: (bi, 0, p)),
                  pl.BlockSpec((None, nkt, 2 * VT_ROWS, tk), lambda bi, p, qi: (bi, 0, p, 0))],
        out_specs=pl.BlockSpec((None, tq, 2 * V_HEAD), lambda bi, p, qi: (bi, qi, p)),
        out_shape=jax.ShapeDtypeStruct((b, s, MLA_WIDTH), F32),
        compiler_params=pltpu.CompilerParams(
            dimension_semantics=("parallel", "parallel", "arbitrary")),
        name="mla_prompt",
    )(qt, k, vt)


def _memkv_body(mem_ref, g_ref, wk_ref, wv_ref, g64_ref, gmk_ref, k_ref, v_ref):
    hm = _rms(mem_ref[...], g_ref[...]).astype(BF16)
    k = _dot(hm, wk_ref[...])
    k_ref[...] = k * lax.rsqrt(_group_mean_sq(k, g64_ref) + EPS) * gmk_ref[...]
    v_ref[...] = _dot(hm, wv_ref[...])


def _mem_kv(mem, w):
    b, n, d = mem.shape
    consts = [w["g_mem"], w["w_mem_k"], w["w_mem_v"], w["G64"], w["gmk_vec"]]
    blk = pl.BlockSpec((None, n, MEM_WIDTH), lambda bi: (bi, 0, 0))
    return pl.pallas_call(
        _memkv_body,
        grid=(b,),
        in_specs=[pl.BlockSpec((None, n, d), lambda bi: (bi, 0, 0))]
        + [_const_spec(c.shape) for c in consts],
        out_specs=(blk, blk),
        out_shape=(jax.ShapeDtypeStruct((b, n, MEM_WIDTH), F32),) * 2,
        compiler_params=pltpu.CompilerParams(dimension_semantics=("parallel",)),
        name="mem_kv",
    )(mem, *consts)


def _out_proj(x, o_mla, y_conv, o_mem, g1_ref, g2_ref, g3_ref, wo_ref):
    o = _dot(_rms(o_mla, g1_ref[...]).astype(BF16), wo_ref[0:MLA_WIDTH, :])
    o += _dot(_rms(y_conv, g2_ref[...]).astype(BF16), wo_ref[MLA_WIDTH:MLA_WIDTH + CONV_WIDTH, :])
    o += _dot(_rms(o_mem, g3_ref[...]).astype(BF16), wo_ref[MLA_WIDTH + CONV_WIDTH:, :])
    return x + o


def _postmix_body(x_ref, omla_ref, u_ref, uprev_ref, gb_ref, mq_ref, mk_ref, mv_ref, cw_ref,
                  g1_ref, g2_ref, g3_ref, wo_ref, o_ref, ubuf):
    tm = x_ref.shape[0]
    n_mem = mv_ref.shape[0] // MEM_HEADS
    u = u_ref[...]
    ubuf[0:8, :] = jnp.where(pl.program_id(1) == 0, 0.0, uprev_ref[...])
    ubuf[8:8 + tm, :] = u
    cw = cw_ref[...]
    y = cw[0:1] * ubuf[pl.ds(6, tm), :] + cw[1:2] * ubuf[pl.ds(7, tm), :] + cw[2:3] * u
    y = gb_ref[...] * y
    s = _dot(mq_ref[...], mk_ref[...])
    ps = []
    for h in range(MEM_HEADS):
        sh = s[:, h * n_mem:(h + 1) * n_mem]
        p = jnp.exp2(sh - jnp.max(sh, axis=-1, keepdims=True))
        ps.append((p / jnp.sum(p, axis=-1, keepdims=True)).astype(BF16))
    o_mem = _dot(jnp.concatenate(ps, axis=1), mv_ref[...])
    o_ref[...] = _out_proj(x_ref[...], omla_ref[...], y, o_mem, g1_ref, g2_ref, g3_ref, wo_ref)


def _postmix(x, o_mla, u, gb, mq, mk_bd, mv_bd, w, tm):
    b, s, d = x.shape
    row = lambda width: pl.BlockSpec((None, tm, width), lambda bi, i: (bi, i, 0))
    prev = pl.BlockSpec((None, 8, CONV_WIDTH), lambda bi, i: (bi, jnp.maximum(i * (tm // 8) - 1, 0), 0))
    per_b = lambda a: pl.BlockSpec((None,) + a.shape[1:], lambda bi, i: (bi, 0, 0))
    consts = [w["conv_w"], w["g_out_mla"], w["g_out_conv"], w["g_out_mem"], w["w_o"]]
    return pl.pallas_call(
        _postmix_body,
        grid=(b, s // tm),
        in_specs=[row(d), row(MLA_WIDTH), row(CONV_WIDTH), prev, row(CONV_WIDTH), row(MEM_WIDTH),
                  per_b(mk_bd), per_b(mv_bd)] + [_const_spec(c.shape) for c in consts],
        out_specs=row(d),
        out_shape=jax.ShapeDtypeStruct((b, s, d), F32),
        scratch_shapes=[pltpu.VMEM((tm + 8, CONV_WIDTH), F32)],
        compiler_params=pltpu.CompilerParams(dimension_semantics=("parallel", "parallel")),
        name="postmix",
    )(x, o_mla, u, u, gb, mq, mk_bd, mv_bd, *consts)


def _absorb_body(q_ref, gk_ref, wt_ref, a_ref):
    for h in range(N_HEADS):
        qh = q_ref[:, h * HEAD_SLOT:(h + 1) * HEAD_SLOT].astype(F32) * gk_ref[...]
        a_ref[h] = _dot(qh.astype(BF16), wt_ref[h])


def _absorb(q, gk_slot, wuk_t):
    n = q.shape[0]
    return pl.pallas_call(
        _absorb_body,
        out_shape=jax.ShapeDtypeStruct((N_HEADS, n, KV_LORA), F32),
        name="absorb_q",
    )(q, gk_slot, wuk_t)


def _paged_body(pt_ref, wt_ref, a_ref, qr_ref, ckv_hbm, kr_hbm, acc_ref, m_ref, l_ref,
                xs, krs, lhs, sem, *, n_chunks, page_base):
    nk = PAGES_PER_CHUNK * PAGE_SIZE
    b = pl.program_id(0)
    total = pl.num_programs(0) * n_chunks

    def copies(t, slot):
        out = []
        for i in range(PAGES_PER_CHUNK):
            pg = pt_ref[t * PAGES_PER_CHUNK + i] + page_base
            rows = pl.ds(i * PAGE_SIZE, PAGE_SIZE)
            out.append(pltpu.make_async_copy(ckv_hbm.at[pg], xs.at[slot, rows, :], sem.at[slot, 0]))
            out.append(pltpu.make_async_copy(kr_hbm.at[pg], krs.at[slot, rows, :], sem.at[slot, 1]))
        return out

    @pl.when(b == 0)
    def _():
        for c in copies(0, 0):
            c.start()

    lhs[0:N_HEADS * QK_NOPE, :] = wt_ref[...]
    lhs[N_HEADS * QK_NOPE:, :] = jnp.concatenate(
        [a_ref[0], jnp.zeros((8, KV_LORA), F32)], axis=0).astype(BF16)
    qr = jnp.concatenate([qr_ref[0], jnp.zeros((8, QK_ROPE), F32)], axis=0).astype(BF16)

    def chunk(c, carry):
        m, l, acc = carry
        t = b * n_chunks + c
        slot = t % 2

        @pl.when(t + 1 < total)
        def _():
            for cp in copies(t + 1, 1 - slot):
                cp.start()

        for cp in copies(t, slot):
            cp.wait()
        x16 = xs[slot].astype(BF16)
        kr16 = krs[slot].astype(BF16)
        out = _dot_nt(lhs[...], x16)
        k_t = out[0:N_HEADS * QK_NOPE]
        n = jnp.sum((k_t * k_t).reshape(N_HEADS, QK_NOPE, nk), axis=1)
        s = (out[N_HEADS * QK_NOPE:N_HEADS * QK_NOPE + N_HEADS] * lax.rsqrt(n * (1.0 / QK_NOPE) + EPS)
             + _dot_nt(qr, kr16)[0:N_HEADS])
        m_new = jnp.maximum(m, jnp.max(s, axis=-1, keepdims=True))
        alpha = jnp.exp2(m - m_new)
        p = jnp.exp2(s - m_new)
        l = alpha * l + jnp.sum(p, axis=-1, keepdims=True)
        p16 = jnp.concatenate([p, jnp.zeros_like(p)], axis=0).astype(BF16)
        acc = alpha * acc + _dot(p16, x16)[0:N_HEADS]
        return m_new, l, acc

    init = (jnp.full((N_HEADS, 1), NEG, F32), jnp.zeros((N_HEADS, 1), F32),
            jnp.zeros((N_HEADS, KV_LORA), F32))
    m, l, acc = lax.fori_loop(0, n_chunks, chunk, init)
    acc_ref[0] = acc
    m_ref[0] = jnp.broadcast_to(m, (N_HEADS, HEAD_SLOT))
    l_ref[0] = jnp.broadcast_to(l, (N_HEADS, HEAD_SLOT))


def _paged_mla(page_table, wuk_t2d, a, qr, cache_ckv, cache_krope, page_base):
    nb, n_pages = page_table.shape
    n_chunks = n_pages // PAGES_PER_CHUNK
    nk = PAGES_PER_CHUNK * PAGE_SIZE
    per_b = lambda width: pl.BlockSpec((1, N_HEADS, width), lambda bi, pt: (bi, 0, 0))
    grid_spec = pltpu.PrefetchScalarGridSpec(
        num_scalar_prefetch=1,
        grid=(nb,),
        in_specs=[pl.BlockSpec(wuk_t2d.shape, lambda bi, pt: (0, 0)),
                  per_b(KV_LORA), per_b(QK_ROPE),
                  pl.BlockSpec(memory_space=pl.ANY), pl.BlockSpec(memory_space=pl.ANY)],
        out_specs=(per_b(KV_LORA), per_b(HEAD_SLOT), per_b(HEAD_SLOT)),
        scratch_shapes=[pltpu.VMEM((2, nk, KV_LORA), F32),
                        pltpu.VMEM((2, nk, QK_ROPE), F32),
                        pltpu.VMEM((N_HEADS * QK_NOPE + 16, KV_LORA), BF16),
                        pltpu.SemaphoreType.DMA((2, 2))],
    )
    out = jax.ShapeDtypeStruct((nb, N_HEADS, HEAD_SLOT), F32)
    return pl.pallas_call(
        functools.partial(_paged_body, n_chunks=n_chunks, page_base=page_base),
        grid_spec=grid_spec,
        out_shape=(out, out, out),
        compiler_params=pltpu.CompilerParams(dimension_semantics=("arbitrary",)),
        name="paged_mla",
    )(page_table.reshape(-1), wuk_t2d, a, qr, cache_ckv, cache_krope)


def _smem_body(mq_ref, mk_ref, mv_ref, gsel_ref, gselt_ref, o_ref):
    rows = []
    for i in range(mk_ref.shape[0]):
        prod = mk_ref[i] * mq_ref[i:i + 1, :].astype(F32)
        hi = prod.astype(BF16)
        lo = (prod - hi.astype(F32)).astype(BF16)
        s = _dot(hi, gsel_ref[...]) + _dot(lo, gsel_ref[...])
        p = jnp.exp2(s - jnp.max(s, axis=0, keepdims=True))
        p = (p / jnp.sum(p, axis=0, keepdims=True)).astype(BF16)
        pe = _dot(p, gselt_ref[...])
        rows.append(jnp.sum(pe * mv_ref[i], axis=0, keepdims=True))
    o_ref[...] = jnp.concatenate(rows, axis=0)


def _sample_mem_attend(mq, mk, mv, gsel, gselt, bc=8):
    nb, n_mem, w = mk.shape
    return pl.pallas_call(
        _smem_body,
        grid=(nb // bc,),
        in_specs=[pl.BlockSpec((bc, w), lambda i: (i, 0)),
                  pl.BlockSpec((bc, n_mem, w), lambda i: (i, 0, 0)),
                  pl.BlockSpec((bc, n_mem, w), lambda i: (i, 0, 0)),
                  _const_spec(gsel.shape), _const_spec(gselt.shape)],
        out_specs=pl.BlockSpec((bc, w), lambda i: (i, 0)),
        out_shape=jax.ShapeDtypeStruct((nb, w), F32),
        compiler_params=pltpu.CompilerParams(dimension_semantics=("parallel",)),
        name="sample_mem_attend",
    )(mq, mk, mv, gsel, gselt)


def _spost_body(x_ref, q_ref, k_ref, ckv_ref, acc_ref, m_ref, l_ref, wuv_ref, u_ref, gb_ref,
                s0_ref, s1_ref, cw_ref, omem_ref, g1_ref, g2_ref, g3_ref, wo_ref, o_ref):
    ckv = ckv_ref[...]
    o_mla = jnp.zeros((x_ref.shape[0], MLA_WIDTH), F32)
    for h in range(N_HEADS):
        sl = slice(h * HEAD_SLOT, (h + 1) * HEAD_SLOT)
        s_new = jnp.sum(q_ref[:, sl].astype(F32) * k_ref[:, sl].astype(F32), axis=-1, keepdims=True)
        m_old = m_ref[h][:, 0:1]
        l_old = l_ref[h][:, 0:1]
        m_new = jnp.maximum(m_old, s_new)
        alpha = jnp.exp2(m_old - m_new)
        p_new = jnp.exp2(s_new - m_new)
        o_lat = (acc_ref[h] * alpha + p_new * ckv) / (l_old * alpha + p_new)
        o_mla += _dot(o_lat.astype(BF16), wuv_ref[h])
    cw = cw_ref[...]
    y = gb_ref[...] * (cw[0:1] * s0_ref[...] + cw[1:2] * s1_ref[...] + cw[2:3] * u_ref[...])
    o_ref[...] = _out_proj(x_ref[...], o_mla, y, omem_ref[...], g1_ref, g2_ref, g3_ref, wo_ref)


def _sample_postmix(*args):
    n = args[0].shape[0]
    return pl.pallas_call(
        _spost_body,
        out_shape=jax.ShapeDtypeStruct((n, D_MODEL), F32),
        name="sample_postmix",
    )(*args)


def _rope_tables(pos):
    inv_freq = ROPE_THETA ** (-jnp.arange(0, QK_ROPE, 2, dtype=F32) / QK_ROPE)
    ang = pos.astype(F32)[:, None] * inv_freq[None, :]
    cos, sin = jnp.cos(ang), jnp.sin(ang)
    n = pos.shape[0]
    zeros = lambda w_: jnp.zeros((n, w_), F32)
    tail = HEAD_SLOT - ROPE_LO - QK_ROPE
    c = jnp.concatenate([jnp.ones((n, ROPE_LO), F32), cos, cos, zeros(tail)], axis=1)
    s1 = jnp.concatenate([zeros(ROPE_LO), -sin, zeros(HALF_ROPE), zeros(tail)], axis=1)
    s2 = jnp.concatenate([zeros(ROPE_LO), zeros(HALF_ROPE), sin, zeros(tail)], axis=1)
    return c, s1, s2


def _block_diag_avg(sizes, width):
    idx = jnp.arange(width)
    gid = jnp.full((width,), -1, jnp.int32)
    scale = jnp.zeros((width,), F32)
    lo = 0
    for g, sz in enumerate(sizes):
        inside = (idx >= lo) & (idx < lo + sz)
        gid = jnp.where(inside, g, gid)
        scale = jnp.where(inside, 1.0 / sz, scale)
        lo += sz
    same = (gid[:, None] == gid[None, :]) & (gid[:, None] >= 0)
    return jnp.where(same, scale[None, :], 0.0).astype(BF16)


def _layer_weights(l, p):
    pad_last = lambda a, n: jnp.pad(a, [(0, 0)] * (a.ndim - 1) + [(0, n - a.shape[-1])])
    row = lambda v: v.reshape(1, -1).astype(F32)
    w_in = p["w_in"][l]
    zc = lambda n: jnp.zeros((D_MODEL, n), F32)
    w_in_p = jnp.concatenate([w_in[:, 0:384], zc(ROPE_LO), w_in[:, 384:416],
                              zc(HEAD_SLOT - ROPE_LO - QK_ROPE), w_in[:, 416:]], axis=1)
    slot_gain = lambda nope, rope: jnp.concatenate(
        [nope, rope, jnp.zeros((HEAD_SLOT - QK_NOPE - QK_ROPE,), F32)])
    zeros_n = jnp.zeros((QK_NOPE,), F32)
    zeros_r = jnp.zeros((QK_ROPE,), F32)
    g_kn = p["g_kn"][l]
    w_uk = p["w_uk"][l]
    g_slot = _block_diag_avg((QK_NOPE, QK_ROPE), HEAD_SLOT).astype(F32)
    g2 = jnp.kron(jnp.eye(2, dtype=F32), g_slot).astype(BF16)
    sel = (jnp.arange(MEM_WIDTH)[:, None] // MEM_HEAD_DIM == jnp.arange(HEAD_SLOT)[None, :])
    wuv = p["w_uv"][l]
    wuv_bd = jnp.stack([jnp.pad(wuv[:, h, :], ((0, 0), (h * V_HEAD, MLA_WIDTH - (h + 1) * V_HEAD)))
                        for h in range(N_HEADS)])
    return {
        "g_ffn1": row(p["g_ffn1"][l]), "w1_gate": p["w1_gate"][l].astype(BF16),
        "w1_up": p["w1_up"][l].astype(BF16), "w1_down": p["w1_down"][l].astype(BF16),
        "g_ffn2": row(p["g_ffn2"][l]), "w2_gate": p["w2_gate"][l].astype(BF16),
        "w2_up": p["w2_up"][l].astype(BF16), "w2_down": p["w2_down"][l].astype(BF16),
        "g_mix": row(p["g_mix"][l]), "w_in": w_in_p.astype(BF16),
        "g_q_lora": row(p["g_q_lora"][l]),
        "w_uq": pad_last(p["w_uq"][l].reshape(Q_LORA, N_HEADS, QK_NOPE + QK_ROPE), HEAD_SLOT)
        .reshape(Q_LORA, N_HEADS * HEAD_SLOT).astype(BF16),
        "gq_vec": row(jnp.tile(slot_gain(p["g_qn"][l], p["g_qr"][l]), N_HEADS) * (MLA_SCALE * LOG2E)),
        "G2": g2,
        "g_kv_lora": row(p["g_kv_lora"][l]),
        "gkr_vec": row(slot_gain(zeros_n, p["g_kr"][l])),
        "w_uk": pad_last(w_uk, HEAD_SLOT).reshape(KV_LORA, N_HEADS * HEAD_SLOT).astype(BF16),
        "gk_vec": row(jnp.tile(slot_gain(g_kn, zeros_r), N_HEADS)),
        "gk_slot": row(slot_gain(g_kn, zeros_r)),
        "w_uv": wuv.reshape(KV_LORA, MLA_WIDTH).astype(BF16),
        "wuv_bd": wuv_bd.astype(BF16),
        "wuk_t_pad": jnp.pad(jnp.transpose(w_uk, (1, 2, 0)), ((0, 0), (0, HEAD_SLOT - QK_NOPE), (0, 0)))
        .astype(BF16),
        "wuk_t2d": jnp.transpose(w_uk, (1, 2, 0)).reshape(N_HEADS * QK_NOPE, KV_LORA).astype(BF16),
        "gmq_vec": row(jnp.tile(p["g_mq"][l], MEM_HEADS) * (MEM_SCALE * LOG2E)),
        "G64": _block_diag_avg((MEM_HEAD_DIM,) * MEM_HEADS, MEM_WIDTH),
        "g_mem": row(p["g_mem"][l]), "w_mem_k": p["w_mem_k"][l].astype(BF16),
        "w_mem_v": p["w_mem_v"][l].astype(BF16),
        "gmk_vec": row(jnp.tile(p["g_mk"][l], MEM_HEADS)),
        "gsel": sel.astype(BF16), "gsel_t": sel.T.astype(BF16),
        "conv_w": p["conv_w"][l].astype(F32),
        "g_out_mla": row(p["g_out_mla"][l]), "g_out_conv": row(p["g_out_conv"][l]),
        "g_out_mem": row(p["g_out_mem"][l]), "w_o": p["w_o"][l].astype(BF16),
    }


def _mem_block_diag(mk, mv):
    b, n, _ = mk.shape
    k4 = mk.reshape(b, n, MEM_HEADS, MEM_HEAD_DIM)
    v4 = mv.reshape(b, n, MEM_HEADS, MEM_HEAD_DIM)
    eye = jnp.eye(MEM_HEADS, dtype=F32)
    k_bd = jnp.einsum("bnhd,hg->bhdgn", k4, eye).reshape(b, MEM_WIDTH, MEM_HEADS * n)
    v_bd = jnp.einsum("bnhd,hg->bhngd", v4, eye).reshape(b, MEM_HEADS * n, MEM_WIDTH)
    return k_bd.astype(BF16), v_bd.astype(BF16)


def _prompt_layer(x, mem, w, tables, tm):
    b, s, d = x.shape
    x = _ffn_half(x.reshape(b * s, d), w["g_ffn1"], w["w1_gate"], w["w1_up"], w["w1_down"],
                  tm).reshape(b, s, d)
    qt, k, vt, ckv, kr, u, gb, mq = _premix(x, tables, w, tm)
    o_mla = _mla_prompt(qt, k, vt)
    mk, mv = _mem_kv(mem, w)
    mk_bd, mv_bd = _mem_block_diag(mk, mv)
    x = _postmix(x, o_mla, u, gb, mq, mk_bd, mv_bd, w, tm)
    x = _ffn_half(x.reshape(b * s, d), w["g_ffn2"], w["w2_gate"], w["w2_up"], w["w2_down"],
                  tm).reshape(b, s, d)
    return x, ckv, kr, u, mk, mv


def _sample_layer(x, w, tables, cache_ckv, cache_krope, page_base, page_table, state, mem_k, mem_v):
    n, d = x.shape
    x = _ffn_half(x, w["g_ffn1"], w["w1_gate"], w["w1_up"], w["w1_down"], n)
    qt, k, _, ckv, kr, u, gb, mq = _premix(x[None], tables, w, n)
    q = qt[0].T
    a = _absorb(q, w["gk_slot"], w["wuk_t_pad"])
    qr = q.reshape(n, N_HEADS, HEAD_SLOT)[:, :, ROPE_LO:ROPE_LO + QK_ROPE].astype(F32)
    acc, m, l = _paged_mla(page_table, w["wuk_t2d"], jnp.transpose(a, (1, 0, 2)), qr,
                           cache_ckv, cache_krope, page_base)
    head_major = lambda t: jnp.transpose(t, (1, 0, 2))
    o_mem = _sample_mem_attend(mq[0].astype(F32), mem_k.reshape(n, -1, MEM_WIDTH), mem_v.reshape(n, -1, MEM_WIDTH),
                               w["gsel"], w["gsel_t"])
    x = _sample_postmix(x, q, k[0], ckv[0], head_major(acc), head_major(m), head_major(l),
                        w["wuv_bd"], u[0], gb[0], state[:, 0, :], state[:, 1, :], w["conv_w"], o_mem,
                        w["g_out_mla"], w["g_out_conv"], w["g_out_mem"], w["w_o"])
    x = _ffn_half(x, w["g_ffn2"], w["w2_gate"], w["w2_up"], w["w2_down"], n)
    return x, ckv[0], kr[0], jnp.stack([state[:, 1, :], u[0]], axis=1)


def kernel(x_prompt, mem_prompt, x_sample, cache_ckv, cache_krope, page_table, state_conv, cache_mem_k,
           cache_mem_v, g_ffn1, w1_gate, w1_up, w1_down, g_mix, w_in, g_q_lora, w_uq, g_qn, g_qr,
           g_kv_lora, w_uk, w_uv, g_kn, g_kr, conv_w, g_mem, w_mem_k, w_mem_v, g_mq, g_mk, g_out_mla,
           g_out_conv, g_out_mem, w_o, g_ffn2, w2_gate, w2_up, w2_down):
    params = dict(g_ffn1=g_ffn1, w1_gate=w1_gate, w1_up=w1_up, w1_down=w1_down, g_mix=g_mix, w_in=w_in,
                  g_q_lora=g_q_lora, w_uq=w_uq, g_qn=g_qn, g_qr=g_qr, g_kv_lora=g_kv_lora, w_uk=w_uk,
                  w_uv=w_uv, g_kn=g_kn, g_kr=g_kr, conv_w=conv_w, g_mem=g_mem, w_mem_k=w_mem_k,
                  w_mem_v=w_mem_v, g_mq=g_mq, g_mk=g_mk, g_out_mla=g_out_mla, g_out_conv=g_out_conv,
                  g_out_mem=g_out_mem, w_o=w_o, g_ffn2=g_ffn2, w2_gate=w2_gate, w2_up=w2_up,
                  w2_down=w2_down)
    depth = w_in.shape[0]
    b, s, _ = x_prompt.shape
    nb, dec_seq, _ = x_sample.shape
    assert dec_seq == 1
    n_phys = cache_ckv.shape[1]
    tm = min(512, s)
    tab_p = _rope_tables(jnp.arange(s))
    tab_s = _rope_tables(jnp.full((nb,), PAST_LEN, jnp.int32))
    ckv_pages = cache_ckv.reshape(depth * n_phys, PAGE_SIZE, KV_LORA)
    kr_pages = cache_krope.reshape(depth * n_phys, PAGE_SIZE, QK_ROPE)

    xp, xs = x_prompt, x_sample.reshape(nb, D_MODEL)
    outs_p, outs_s = [], []
    for l in range(depth):
        w = _layer_weights(l, params)
        xp, ckv, kr, u, mk, mv = _prompt_layer(xp, mem_prompt, w, tab_p, tm)
        outs_p.append((ckv, kr, u[:, -(CONV_K - 1):], mk.reshape(b, -1, MEM_HEADS, MEM_HEAD_DIM),
                       mv.reshape(b, -1, MEM_HEADS, MEM_HEAD_DIM)))
        xs, ckv_s, kr_s, conv_s = _sample_layer(xs, w, tab_s, ckv_pages, kr_pages, l * n_phys, page_table,
                                                state_conv[l], cache_mem_k[l], cache_mem_v[l])
        outs_s.append((ckv_s[:, None, :], kr_s[:, None, :], conv_s))
    stack = lambda items, i: jnp.stack([it[i] for it in items])
    return (xp, xs.reshape(nb, 1, D_MODEL), stack(outs_p, 0), stack(outs_p, 1), stack(outs_p, 2),
            stack(outs_p, 3), stack(outs_p, 4), stack(outs_s, 0), stack(outs_s, 1), stack(outs_s, 2))
```

```python
import functools

import jax
import jax.numpy as jnp
from jax import lax
from jax.experimental import pallas as pl
from jax.experimental.pallas import tpu as pltpu

F32 = jnp.float32
BF16 = jnp.bfloat16

D_MODEL = 1024
N_HEADS = 8
Q_LORA = 256
KV_LORA = 128
QK_NOPE = 64
QK_ROPE = 32
V_HEAD = 64
MLA_WIDTH = N_HEADS * V_HEAD
CONV_WIDTH = 256
CONV_K = 3
MEM_HEADS = 4
MEM_HEAD_DIM = 64
MEM_WIDTH = MEM_HEADS * MEM_HEAD_DIM
D_FF = 2816
ROPE_THETA = 10000.0
EPS = 1e-6
PAST_LEN = 16384
PAGE_SIZE = 128
MLA_SCALE = (QK_NOPE + QK_ROPE) ** -0.5
MEM_SCALE = MEM_HEAD_DIM ** -0.5
LOG2E = 1.4426950408889634

HEAD_SLOT = 128
ROPE_LO = QK_NOPE
HALF_ROPE = QK_ROPE // 2
Z_WIDTH = 1536
VT_ROWS = 80
NEG = -1e30
PAGES_PER_CHUNK = 16
PAGE_SLOTS = 3
KEY_SUB = 256


def _rms(x, g):
    ms = jnp.mean(x * x, axis=-1, keepdims=True)
    return x * lax.rsqrt(ms + EPS) * g


def _dot(a, b):
    return jnp.dot(a, b, preferred_element_type=F32)


def _dot_nt(a, b):
    return lax.dot_general(a, b, (((1,), (1,)), ((), ())), preferred_element_type=F32)


def _group_mean_sq(x, g_ref):
    x2 = x * x
    hi = x2.astype(BF16)
    lo = (x2 - hi.astype(F32)).astype(BF16)
    g = g_ref[...]
    cols = []
    for j in range(x.shape[1] // 256):
        sl = slice(j * 256, (j + 1) * 256)
        cols.append(_dot(hi[:, sl], g) + _dot(lo[:, sl], g))
    return cols[0] if len(cols) == 1 else jnp.concatenate(cols, axis=1)


def _rope_slab(v, cos, s1, s2):
    return (v * cos + pltpu.roll(v, HEAD_SLOT - HALF_ROPE, 1) * s1
            + pltpu.roll(v, HALF_ROPE, 1) * s2)


def _const_spec(shape):
    nd = len(shape)
    return pl.BlockSpec(shape, lambda *_: (0,) * nd, pipeline_mode=pl.Buffered(1))


def _ffn_body(x_ref, g_ref, wg_ref, wu_ref, wd_ref, o_ref):
    x = x_ref[...]
    xn = _rms(x, g_ref[...]).astype(BF16)
    h = _dot(xn, wg_ref[...])
    u = _dot(xn, wu_ref[...])
    a = (h / (1.0 + jnp.exp(-h)) * u).astype(BF16)
    o_ref[...] = x + 0.5 * _dot(a, wd_ref[...])


def _ffn_half(x, g, wg, wu, wd, tm):
    n, d = x.shape
    dff = wg.shape[1]
    return pl.pallas_call(
        _ffn_body,
        grid=(n // tm,),
        in_specs=[pl.BlockSpec((tm, d), lambda i: (i, 0)),
                  _const_spec((1, d)), _const_spec((d, dff)), _const_spec((d, dff)),
                  _const_spec((dff, d))],
        out_specs=pl.BlockSpec((tm, d), lambda i: (i, 0)),
        out_shape=jax.ShapeDtypeStruct((n, d), F32),
        compiler_params=pltpu.CompilerParams(dimension_semantics=("parallel",)),
        name="ffn_half",
    )(x, g, wg, wu, wd)


def _premix_body(x_ref, cos_ref, s1_ref, s2_ref, gmix_ref, win_ref, gql_ref, wuq_ref, gq_ref,
                 g2_ref, gkvl_ref, gkr_ref, wuk_ref, gk_ref, wuv_ref, gmq_ref, g64_ref,
                 qt_ref, k_ref, vt_ref, ckv_ref, kr_ref, u_ref, gb_ref, mq_ref):
    tm = x_ref.shape[0]
    hn = _rms(x_ref[...], gmix_ref[...]).astype(BF16)
    z = _dot(hn, win_ref[...])
    c_q, c_kv, k_r = z[:, 0:256], z[:, 256:384], z[:, 384:512]
    u_in, g_b, g_c, m_q = z[:, 512:768], z[:, 768:1024], z[:, 1024:1280], z[:, 1280:1536]
    cos, s1, s2 = cos_ref[...], s1_ref[...], s2_ref[...]

    q = _dot(_rms(c_q, gql_ref[...]).astype(BF16), wuq_ref[...])
    q = q * lax.rsqrt(_group_mean_sq(q, g2_ref) + EPS) * gq_ref[...]
    q = jnp.concatenate(
        [_rope_slab(q[:, h * HEAD_SLOT:(h + 1) * HEAD_SLOT], cos, s1, s2) for h in range(N_HEADS)],
        axis=1)
    qt_ref[...] = q.T.astype(BF16)

    ckv = _rms(c_kv, gkvl_ref[...])
    ckv_ref[...] = ckv
    ckv16 = ckv.astype(BF16)
    kr_ms = jnp.sum(k_r * k_r, axis=-1, keepdims=True) * (1.0 / QK_ROPE)
    krr = _rope_slab(k_r * lax.rsqrt(kr_ms + EPS) * gkr_ref[...], cos, s1, s2)
    kr_ref[...] = krr[:, ROPE_LO:ROPE_LO + QK_ROPE]
    kk = _dot(ckv16, wuk_ref[...])
    kk = kk * lax.rsqrt(_group_mean_sq(kk, g2_ref) + EPS) * gk_ref[...]
    k_ref[...] = jnp.concatenate(
        [kk[:, h * HEAD_SLOT:(h + 1) * HEAD_SLOT] + krr for h in range(N_HEADS)],
        axis=1).astype(BF16)
    vt = _dot(ckv16, wuv_ref[...]).T
    tkv = vt_ref.shape[2]
    ones_rows = jnp.where(lax.broadcasted_iota(jnp.int32, (VT_ROWS - V_HEAD, tkv), 0) == 0,
                          1.0, 0.0).astype(BF16)
    for u in range(tm // tkv):
        for h in range(N_HEADS):
            vt_ref[u, h * VT_ROWS:h * VT_ROWS + V_HEAD, :] = (
                vt[h * V_HEAD:(h + 1) * V_HEAD, u * tkv:(u + 1) * tkv].astype(BF16))
            vt_ref[u, h * VT_ROWS + V_HEAD:(h + 1) * VT_ROWS, :] = ones_rows

    u_ref[...] = g_c * u_in
    gb_ref[...] = g_b
    mq = m_q * lax.rsqrt(_group_mean_sq(m_q, g64_ref) + EPS) * gmq_ref[...]
    mq_ref[...] = mq.astype(BF16)


def _premix(x, tables, w, tm):
    b, s, d = x.shape
    tkv = min(KEY_SUB, tm)
    cos, s1, s2 = tables
    row = lambda width: pl.BlockSpec((None, tm, width), lambda bi, i: (bi, i, 0))
    tab = pl.BlockSpec((tm, HEAD_SLOT), lambda bi, i: (i, 0))
    consts = [w["g_mix"], w["w_in"], w["g_q_lora"], w["w_uq"], w["gq_vec"], w["G2"], w["g_kv_lora"],
              w["gkr_vec"], w["w_uk"], w["gk_vec"], w["w_uv"], w["gmq_vec"], w["G64"]]
    out_shape = (
        jax.ShapeDtypeStruct((b, N_HEADS * HEAD_SLOT, s), BF16),
        jax.ShapeDtypeStruct((b, s, N_HEADS * HEAD_SLOT), BF16),
        jax.ShapeDtypeStruct((b, s // tkv, N_HEADS * VT_ROWS, tkv), BF16),
        jax.ShapeDtypeStruct((b, s, KV_LORA), F32),
        jax.ShapeDtypeStruct((b, s, QK_ROPE), F32),
        jax.ShapeDtypeStruct((b, s, CONV_WIDTH), F32),
        jax.ShapeDtypeStruct((b, s, CONV_WIDTH), F32),
        jax.ShapeDtypeStruct((b, s, MEM_WIDTH), BF16),
    )
    out_specs = (
        pl.BlockSpec((None, N_HEADS * HEAD_SLOT, tm), lambda bi, i: (bi, 0, i)),
        row(N_HEADS * HEAD_SLOT),
        pl.BlockSpec((None, tm // tkv, N_HEADS * VT_ROWS, tkv), lambda bi, i: (bi, i, 0, 0)),
        row(KV_LORA), row(QK_ROPE), row(CONV_WIDTH), row(CONV_WIDTH), row(MEM_WIDTH),
    )
    return pl.pallas_call(
        _premix_body,
        grid=(b, s // tm),
        in_specs=[row(d), tab, tab, tab] + [_const_spec(c.shape) for c in consts],
        out_specs=out_specs,
        out_shape=out_shape,
        compiler_params=pltpu.CompilerParams(dimension_semantics=("parallel", "parallel")),
        name="premix",
    )(x, cos, s1, s2, *consts)


def _attn_body(qt_ref, k_ref, vt_ref, o_ref, s_even, s_odd):
    tq = qt_ref.shape[1]
    tk = vt_ref.shape[2]
    assert tq == 2 * tk
    qi = pl.program_id(2)
    q_t = [qt_ref[j * HEAD_SLOT:(j + 1) * HEAD_SLOT, :] for j in range(2)]
    kpos = lax.broadcasted_iota(jnp.int32, (tk, tq), 0)
    qpos = lax.broadcasted_iota(jnp.int32, (tk, tq), 1)

    def scores(i, s_ref):
        rows = pl.ds(pl.multiple_of(i * tk, tk), tk)
        for j in range(2):
            s_ref[j] = _dot(k_ref[rows, j * HEAD_SLOT:(j + 1) * HEAD_SLOT], q_t[j])

    def absorb(i, s_ref, state, diag_offset=None):
        out = []
        for j in range(2):
            m, acc = state[2 * j], state[2 * j + 1]
            s_t = s_ref[j]
            if diag_offset is not None:
                s_t = jnp.where(kpos + diag_offset <= qpos, s_t, NEG)
            m_new = jnp.maximum(m, jnp.max(s_t, axis=0, keepdims=True))
            alpha = jnp.exp2(m - m_new)
            p = jnp.exp2(s_t - m_new).astype(BF16)
            v = vt_ref[i, j * VT_ROWS:(j + 1) * VT_ROWS, :]
            out += [m_new, alpha * acc + _dot(v, p)]
        return tuple(out)

    def pair(g, state):
        scores(2 * g + 1, s_odd)
        state = absorb(2 * g, s_even, state)
        scores(2 * g + 2, s_even)
        return absorb(2 * g + 1, s_odd, state)

    scores(0, s_even)
    init = (jnp.full((1, tq), NEG, F32), jnp.zeros((VT_ROWS, tq), F32)) * 2
    state = lax.fori_loop(0, qi, pair, init)
    scores(2 * qi + 1, s_odd)
    state = absorb(2 * qi, s_even, state, 0)
    state = absorb(2 * qi + 1, s_odd, state, tk)
    outs = [state[2 * j + 1][0:V_HEAD] / state[2 * j + 1][V_HEAD:V_HEAD + 1] for j in range(2)]
    o_ref[...] = jnp.concatenate(outs, axis=0).T


def _mla_prompt(qt, k, vt, tq):
    b, _, s = qt.shape
    nkt, tk = vt.shape[1], vt.shape[3]
    return pl.pallas_call(
        _attn_body,
        grid=(b, N_HEADS // 2, s // tq),
        in_specs=[pl.BlockSpec((None, 2 * HEAD_SLOT, tq), lambda bi, p, qi: (bi, p, qi)),
                  pl.BlockSpec((None, s, 2 * HEAD_SLOT), lambda bi, p, qi: (bi, 0, p)),
                  pl.BlockSpec((None, nkt, 2 * VT_ROWS, tk), lambda bi, p, qi: (bi, 0, p, 0))],
        out_specs=pl.BlockSpec((None, tq, 2 * V_HEAD), lambda bi, p, qi: (bi, qi, p)),
        out_shape=jax.ShapeDtypeStruct((b, s, MLA_WIDTH), F32),
        scratch_shapes=[pltpu.VMEM((2, tk, tq), F32), pltpu.VMEM((2, tk, tq), F32)],
        compiler_params=pltpu.CompilerParams(
            dimension_semantics=("parallel", "parallel", "arbitrary")),
        name="mla_prompt",
    )(qt, k, vt)


def _memkv_body(mem_ref, g_ref, wk_ref, wv_ref, g64_ref, gmk_ref, k_ref, v_ref):
    hm = _rms(mem_ref[...], g_ref[...]).astype(BF16)
    k = _dot(hm, wk_ref[...])
    k_ref[...] = k * lax.rsqrt(_group_mean_sq(k, g64_ref) + EPS) * gmk_ref[...]
    v_ref[...] = _dot(hm, wv_ref[...])


def _mem_kv(mem, w):
    b, n, d = mem.shape
    consts = [w["g_mem"], w["w_mem_k"], w["w_mem_v"], w["G64"], w["gmk_vec"]]
    blk = pl.BlockSpec((None, n, MEM_WIDTH), lambda bi: (bi, 0, 0))
    return pl.pallas_call(
        _memkv_body,
        grid=(b,),
        in_specs=[pl.BlockSpec((None, n, d), lambda bi: (bi, 0, 0))]
        + [_const_spec(c.shape) for c in consts],
        out_specs=(blk, blk),
        out_shape=(jax.ShapeDtypeStruct((b, n, MEM_WIDTH), F32),) * 2,
        compiler_params=pltpu.CompilerParams(dimension_semantics=("parallel",)),
        name="mem_kv",
    )(mem, *consts)


def _out_proj(x, o_mla, y_conv, o_mem, g1_ref, g2_ref, g3_ref, wo_ref):
    o = _dot(_rms(o_mla, g1_ref[...]).astype(BF16), wo_ref[0:MLA_WIDTH, :])
    o += _dot(_rms(y_conv, g2_ref[...]).astype(BF16), wo_ref[MLA_WIDTH:MLA_WIDTH + CONV_WIDTH, :])
    o += _dot(_rms(o_mem, g3_ref[...]).astype(BF16), wo_ref[MLA_WIDTH + CONV_WIDTH:, :])
    return x + o


def _postmix_body(x_ref, omla_ref, u_ref, uprev_ref, gb_ref, mq_ref, mk_ref, mv_ref, cw_ref,
                  g1_ref, g2_ref, g3_ref, wo_ref, o_ref, ubuf):
    tm = x_ref.shape[0]
    n_mem = mv_ref.shape[0] // MEM_HEADS
    u = u_ref[...]
    ubuf[0:8, :] = jnp.where(pl.program_id(1) == 0, 0.0, uprev_ref[...])
    ubuf[8:8 + tm, :] = u
    cw = cw_ref[...]
    y = cw[0:1] * ubuf[pl.ds(6, tm), :] + cw[1:2] * ubuf[pl.ds(7, tm), :] + cw[2:3] * u
    y = gb_ref[...] * y
    s = _dot(mq_ref[...], mk_ref[...])
    ps = []
    for h in range(MEM_HEADS):
        sh = s[:, h * n_mem:(h + 1) * n_mem]
        p = jnp.exp2(sh - jnp.max(sh, axis=-1, keepdims=True))
        ps.append((p / jnp.sum(p, axis=-1, keepdims=True)).astype(BF16))
    o_mem = _dot(jnp.concatenate(ps, axis=1), mv_ref[...])
    o_ref[...] = _out_proj(x_ref[...], omla_ref[...], y, o_mem, g1_ref, g2_ref, g3_ref, wo_ref)


def _postmix(x, o_mla, u, gb, mq, mk_bd, mv_bd, w, tm):
    b, s, d = x.shape
    row = lambda width: pl.BlockSpec((None, tm, width), lambda bi, i: (bi, i, 0))
    prev = pl.BlockSpec((None, 8, CONV_WIDTH), lambda bi, i: (bi, jnp.maximum(i * (tm // 8) - 1, 0), 0))
    per_b = lambda a: pl.BlockSpec((None,) + a.shape[1:], lambda bi, i: (bi, 0, 0))
    consts = [w["conv_w"], w["g_out_mla"], w["g_out_conv"], w["g_out_mem"], w["w_o"]]
    return pl.pallas_call(
        _postmix_body,
        grid=(b, s // tm),
        in_specs=[row(d), row(MLA_WIDTH), row(CONV_WIDTH), prev, row(CONV_WIDTH), row(MEM_WIDTH),
                  per_b(mk_bd), per_b(mv_bd)] + [_const_spec(c.shape) for c in consts],
        out_specs=row(d),
        out_shape=jax.ShapeDtypeStruct((b, s, d), F32),
        scratch_shapes=[pltpu.VMEM((tm + 8, CONV_WIDTH), F32)],
        compiler_params=pltpu.CompilerParams(dimension_semantics=("parallel", "parallel")),
        name="postmix",
    )(x, o_mla, u, u, gb, mq, mk_bd, mv_bd, *consts)


def _absorb_body(q_ref, gk_ref, wt_ref, a_ref):
    for h in range(N_HEADS):
        qh = q_ref[:, h * HEAD_SLOT:(h + 1) * HEAD_SLOT].astype(F32) * gk_ref[...]
        a_ref[h] = _dot(qh.astype(BF16), wt_ref[h])


def _absorb(q, gk_slot, wuk_t):
    n = q.shape[0]
    return pl.pallas_call(
        _absorb_body,
        out_shape=jax.ShapeDtypeStruct((N_HEADS, n, KV_LORA), F32),
        name="absorb_q",
    )(q, gk_slot, wuk_t)


def _paged_body(pt_ref, wt_ref, a_ref, qr_ref, ckv_hbm, kr_hbm, acc_ref, m_ref, l_ref,
                xs, krs, lhs, s_scr, sem, *, n_chunks, ppc, page_base):
    nk = ppc * PAGE_SIZE
    nope = N_HEADS * QK_NOPE
    b = pl.program_id(0)
    last = pl.num_programs(0) * n_chunks - 1

    def copies(t, slot):
        out = []
        for i in range(ppc):
            pg = pt_ref[t * ppc + i] + page_base
            keys = pl.ds(i * PAGE_SIZE, PAGE_SIZE)
            out.append(pltpu.make_async_copy(ckv_hbm.at[pg], xs.at[slot, keys, :], sem.at[slot, 0]))
            out.append(pltpu.make_async_copy(kr_hbm.at[pg], krs.at[slot, :, keys], sem.at[slot, 1]))
        return out

    def start(t, slot):
        for cp in copies(t, slot):
            cp.start()

    def wait(t, slot):
        for cp in copies(t, slot):
            cp.wait()

    def scores(t, slot):
        e = t // n_chunks
        wait(t, slot)
        lhs[nope:, :] = jnp.concatenate([a_ref[e], jnp.zeros((8, KV_LORA), F32)], axis=0).astype(BF16)
        qr = jnp.concatenate([qr_ref[e], jnp.zeros((8, QK_ROPE), F32)], axis=0).astype(BF16)
        out = _dot_nt(lhs[...], xs[slot].astype(BF16))
        k_t = out[0:nope]
        n = jnp.sum((k_t * k_t).reshape(N_HEADS, QK_NOPE, nk), axis=1)
        rope = _dot(qr, krs[slot].astype(BF16))[0:N_HEADS]
        return out[nope:nope + N_HEADS] * lax.rsqrt(n * (1.0 / QK_NOPE) + EPS) + rope

    @pl.when(b == 0)
    def _():
        lhs[0:nope, :] = wt_ref[...]
        start(0, 0)
        start(1, 1)
        s_scr[...] = scores(0, 0)

    def step(c, carry):
        m, l, acc, s = carry
        t = b * n_chunks + c
        start(jnp.minimum(t + 2, last), (t + 2) % PAGE_SLOTS)
        s_next = scores(jnp.minimum(t + 1, last), (t + 1) % PAGE_SLOTS)
        m_new = jnp.maximum(m, jnp.max(s, axis=-1, keepdims=True))
        alpha = jnp.exp2(m - m_new)
        p = jnp.exp2(s - m_new)
        l = alpha * l + jnp.sum(p, axis=-1, keepdims=True)
        p16 = jnp.concatenate([p, jnp.zeros_like(p)], axis=0).astype(BF16)
        acc = alpha * acc + _dot(p16, xs[t % PAGE_SLOTS].astype(BF16))[0:N_HEADS]
        return m_new, l, acc, s_next

    init = (jnp.full((N_HEADS, 1), NEG, F32), jnp.zeros((N_HEADS, 1), F32),
            jnp.zeros((N_HEADS, KV_LORA), F32), s_scr[...])
    m, l, acc, s = lax.fori_loop(0, n_chunks, step, init)
    s_scr[...] = s
    acc_ref[0] = acc
    m_ref[0] = jnp.broadcast_to(m, (N_HEADS, HEAD_SLOT))
    l_ref[0] = jnp.broadcast_to(l, (N_HEADS, HEAD_SLOT))

    @pl.when(b == pl.num_programs(0) - 1)
    def _():
        wait(last, (last + 2) % PAGE_SLOTS)


def _paged_mla(page_table, wuk_t2d, a, qr, cache_ckv, cache_krope, page_base):
    nb, n_pages = page_table.shape
    ppc = min(PAGES_PER_CHUNK, n_pages)
    n_chunks = n_pages // ppc
    assert n_chunks * ppc == n_pages and nb * n_chunks >= 2
    nk = ppc * PAGE_SIZE
    vmem = pl.BlockSpec(memory_space=pltpu.VMEM)
    per_b = pl.BlockSpec((1, N_HEADS, HEAD_SLOT), lambda bi: (bi, 0, 0))
    out = jax.ShapeDtypeStruct((nb, N_HEADS, HEAD_SLOT), F32)
    return pl.pallas_call(
        functools.partial(_paged_body, n_chunks=n_chunks, ppc=ppc, page_base=page_base),
        grid=(nb,),
        in_specs=[pl.BlockSpec(memory_space=pltpu.SMEM), vmem, vmem, vmem,
                  pl.BlockSpec(memory_space=pl.ANY), pl.BlockSpec(memory_space=pl.ANY)],
        out_specs=(per_b, per_b, per_b),
        out_shape=(out, out, out),
        scratch_shapes=[pltpu.VMEM((PAGE_SLOTS, nk, KV_LORA), F32),
                        pltpu.VMEM((PAGE_SLOTS, QK_ROPE, nk), F32),
                        pltpu.VMEM((N_HEADS * QK_NOPE + 16, KV_LORA), BF16),
                        pltpu.VMEM((N_HEADS, nk), F32),
                        pltpu.SemaphoreType.DMA((PAGE_SLOTS, 2))],
        compiler_params=pltpu.CompilerParams(dimension_semantics=("arbitrary",)),
        name="paged_mla",
    )(page_table.reshape(-1), wuk_t2d, a, qr, cache_ckv, cache_krope)


def _smem_body(mq_ref, mk_ref, mv_ref, gsel_ref, gselt_ref, o_ref):
    rows = []
    for i in range(mk_ref.shape[0]):
        prod = mk_ref[i] * mq_ref[i:i + 1, :].astype(F32)
        hi = prod.astype(BF16)
        lo = (prod - hi.astype(F32)).astype(BF16)
        s = _dot(hi, gsel_ref[...]) + _dot(lo, gsel_ref[...])
        p = jnp.exp2(s - jnp.max(s, axis=0, keepdims=True))
        p = (p / jnp.sum(p, axis=0, keepdims=True)).astype(BF16)
        pe = _dot(p, gselt_ref[...])
        rows.append(jnp.sum(pe * mv_ref[i], axis=0, keepdims=True))
    o_ref[...] = jnp.concatenate(rows, axis=0)


def _sample_mem_attend(mq, mk, mv, gsel, gselt, bc=8):
    nb, n_mem, w = mk.shape
    return pl.pallas_call(
        _smem_body,
        grid=(nb // bc,),
        in_specs=[pl.BlockSpec((bc, w), lambda i: (i, 0)),
                  pl.BlockSpec((bc, n_mem, w), lambda i: (i, 0, 0)),
                  pl.BlockSpec((bc, n_mem, w), lambda i: (i, 0, 0)),
                  _const_spec(gsel.shape), _const_spec(gselt.shape)],
        out_specs=pl.BlockSpec((bc, w), lambda i: (i, 0)),
        out_shape=jax.ShapeDtypeStruct((nb, w), F32),
        compiler_params=pltpu.CompilerParams(dimension_semantics=("parallel",)),
        name="sample_mem_attend",
    )(mq, mk, mv, gsel, gselt)


def _spost_body(x_ref, q_ref, k_ref, ckv_ref, acc_ref, m_ref, l_ref, wuv_ref, u_ref, gb_ref,
                s0_ref, s1_ref, cw_ref, omem_ref, g1_ref, g2_ref, g3_ref, wo_ref, o_ref):
    ckv = ckv_ref[...]
    o_mla = jnp.zeros((x_ref.shape[0], MLA_WIDTH), F32)
    for h in range(N_HEADS):
        sl = slice(h * HEAD_SLOT, (h + 1) * HEAD_SLOT)
        s_new = jnp.sum(q_ref[:, sl].astype(F32) * k_ref[:, sl].astype(F32), axis=-1, keepdims=True)
        m_old = m_ref[h][:, 0:1]
        l_old = l_ref[h][:, 0:1]
        m_new = jnp.maximum(m_old, s_new)
        alpha = jnp.exp2(m_old - m_new)
        p_new = jnp.exp2(s_new - m_new)
        o_lat = (acc_ref[h] * alpha + p_new * ckv) / (l_old * alpha + p_new)
        o_mla += _dot(o_lat.astype(BF16), wuv_ref[h])
    cw = cw_ref[...]
    y = gb_ref[...] * (cw[0:1] * s0_ref[...] + cw[1:2] * s1_ref[...] + cw[2:3] * u_ref[...])
    o_ref[...] = _out_proj(x_ref[...], o_mla, y, omem_ref[...], g1_ref, g2_ref, g3_ref, wo_ref)


def _sample_postmix(*args):
    n = args[0].shape[0]
    return pl.pallas_call(
        _spost_body,
        out_shape=jax.ShapeDtypeStruct((n, D_MODEL), F32),
        name="sample_postmix",
    )(*args)


def _rope_tables(pos):
    inv_freq = ROPE_THETA ** (-jnp.arange(0, QK_ROPE, 2, dtype=F32) / QK_ROPE)
    ang = pos.astype(F32)[:, None] * inv_freq[None, :]
    cos, sin = jnp.cos(ang), jnp.sin(ang)
    n = pos.shape[0]
    zeros = lambda w_: jnp.zeros((n, w_), F32)
    tail = HEAD_SLOT - ROPE_LO - QK_ROPE
    c = jnp.concatenate([jnp.ones((n, ROPE_LO), F32), cos, cos, zeros(tail)], axis=1)
    s1 = jnp.concatenate([zeros(ROPE_LO), -sin, zeros(HALF_ROPE), zeros(tail)], axis=1)
    s2 = jnp.concatenate([zeros(ROPE_LO), zeros(HALF_ROPE), sin, zeros(tail)], axis=1)
    return c, s1, s2


def _block_diag_avg(sizes, width):
    idx = jnp.arange(width)
    gid = jnp.full((width,), -1, jnp.int32)
    scale = jnp.zeros((width,), F32)
    lo = 0
    for g, sz in enumerate(sizes):
        inside = (idx >= lo) & (idx < lo + sz)
        gid = jnp.where(inside, g, gid)
        scale = jnp.where(inside, 1.0 / sz, scale)
        lo += sz
    same = (gid[:, None] == gid[None, :]) & (gid[:, None] >= 0)
    return jnp.where(same, scale[None, :], 0.0).astype(BF16)


def _layer_weights(l, p):
    pad_last = lambda a, n: jnp.pad(a, [(0, 0)] * (a.ndim - 1) + [(0, n - a.shape[-1])])
    row = lambda v: v.reshape(1, -1).astype(F32)
    w_in = p["w_in"][l]
    zc = lambda n: jnp.zeros((D_MODEL, n), F32)
    w_in_p = jnp.concatenate([w_in[:, 0:384], zc(ROPE_LO), w_in[:, 384:416],
                              zc(HEAD_SLOT - ROPE_LO - QK_ROPE), w_in[:, 416:]], axis=1)
    slot_gain = lambda nope, rope: jnp.concatenate(
        [nope, rope, jnp.zeros((HEAD_SLOT - QK_NOPE - QK_ROPE,), F32)])
    zeros_n = jnp.zeros((QK_NOPE,), F32)
    zeros_r = jnp.zeros((QK_ROPE,), F32)
    g_kn = p["g_kn"][l]
    w_uk = p["w_uk"][l]
    g_slot = _block_diag_avg((QK_NOPE, QK_ROPE), HEAD_SLOT).astype(F32)
    g2 = jnp.kron(jnp.eye(2, dtype=F32), g_slot).astype(BF16)
    sel = (jnp.arange(MEM_WIDTH)[:, None] // MEM_HEAD_DIM == jnp.arange(HEAD_SLOT)[None, :])
    wuv = p["w_uv"][l]
    wuv_bd = jnp.stack([jnp.pad(wuv[:, h, :], ((0, 0), (h * V_HEAD, MLA_WIDTH - (h + 1) * V_HEAD)))
                        for h in range(N_HEADS)])
    return {
        "g_ffn1": row(p["g_ffn1"][l]), "w1_gate": p["w1_gate"][l].astype(BF16),
        "w1_up": p["w1_up"][l].astype(BF16), "w1_down": p["w1_down"][l].astype(BF16),
        "g_ffn2": row(p["g_ffn2"][l]), "w2_gate": p["w2_gate"][l].astype(BF16),
        "w2_up": p["w2_up"][l].astype(BF16), "w2_down": p["w2_down"][l].astype(BF16),
        "g_mix": row(p["g_mix"][l]), "w_in": w_in_p.astype(BF16),
        "g_q_lora": row(p["g_q_lora"][l]),
        "w_uq": pad_last(p["w_uq"][l].reshape(Q_LORA, N_HEADS, QK_NOPE + QK_ROPE), HEAD_SLOT)
        .reshape(Q_LORA, N_HEADS * HEAD_SLOT).astype(BF16),
        "gq_vec": row(jnp.tile(slot_gain(p["g_qn"][l], p["g_qr"][l]), N_HEADS) * (MLA_SCALE * LOG2E)),
        "G2": g2,
        "g_kv_lora": row(p["g_kv_lora"][l]),
        "gkr_vec": row(slot_gain(zeros_n, p["g_kr"][l])),
        "w_uk": pad_last(w_uk, HEAD_SLOT).reshape(KV_LORA, N_HEADS * HEAD_SLOT).astype(BF16),
        "gk_vec": row(jnp.tile(slot_gain(g_kn, zeros_r), N_HEADS)),
        "gk_slot": row(slot_gain(g_kn, zeros_r)),
        "w_uv": wuv.reshape(KV_LORA, MLA_WIDTH).astype(BF16),
        "wuv_bd": wuv_bd.astype(BF16),
        "wuk_t_pad": jnp.pad(jnp.transpose(w_uk, (1, 2, 0)), ((0, 0), (0, HEAD_SLOT - QK_NOPE), (0, 0)))
        .astype(BF16),
        "wuk_t2d": jnp.transpose(w_uk, (1, 2, 0)).reshape(N_HEADS * QK_NOPE, KV_LORA).astype(BF16),
        "gmq_vec": row(jnp.tile(p["g_mq"][l], MEM_HEADS) * (MEM_SCALE * LOG2E)),
        "G64": _block_diag_avg((MEM_HEAD_DIM,) * MEM_HEADS, MEM_WIDTH),
        "g_mem": row(p["g_mem"][l]), "w_mem_k": p["w_mem_k"][l].astype(BF16),
        "w_mem_v": p["w_mem_v"][l].astype(BF16),
        "gmk_vec": row(jnp.tile(p["g_mk"][l], MEM_HEADS)),
        "gsel": sel.astype(BF16), "gsel_t": sel.T.astype(BF16),
        "conv_w": p["conv_w"][l].astype(F32),
        "g_out_mla": row(p["g_out_mla"][l]), "g_out_conv": row(p["g_out_conv"][l]),
        "g_out_mem": row(p["g_out_mem"][l]), "w_o": p["w_o"][l].astype(BF16),
    }


def _mem_block_diag(mk, mv):
    b, n, _ = mk.shape
    k4 = mk.reshape(b, n, MEM_HEADS, MEM_HEAD_DIM)
    v4 = mv.reshape(b, n, MEM_HEADS, MEM_HEAD_DIM)
    eye = jnp.eye(MEM_HEADS, dtype=F32)
    k_bd = jnp.einsum("bnhd,hg->bhdgn", k4, eye).reshape(b, MEM_WIDTH, MEM_HEADS * n)
    v_bd = jnp.einsum("bnhd,hg->bhngd", v4, eye).reshape(b, MEM_HEADS * n, MEM_WIDTH)
    return k_bd.astype(BF16), v_bd.astype(BF16)


def _prompt_layer(x, mem, w, tables, tm):
    b, s, d = x.shape
    x = _ffn_half(x.reshape(b * s, d), w["g_ffn1"], w["w1_gate"], w["w1_up"], w["w1_down"],
                  tm).reshape(b, s, d)
    qt, k, vt, ckv, kr, u, gb, mq = _premix(x, tables, w, tm)
    o_mla = _mla_prompt(qt, k, vt, tm)
    mk, mv = _mem_kv(mem, w)
    mk_bd, mv_bd = _mem_block_diag(mk, mv)
    x = _postmix(x, o_mla, u, gb, mq, mk_bd, mv_bd, w, tm)
    x = _ffn_half(x.reshape(b * s, d), w["g_ffn2"], w["w2_gate"], w["w2_up"], w["w2_down"],
                  tm).reshape(b, s, d)
    return x, ckv, kr, u, mk, mv


def _sample_layer(x, w, tables, cache_ckv, cache_krope, page_base, page_table, state, mem_k, mem_v):
    n, d = x.shape
    x = _ffn_half(x, w["g_ffn1"], w["w1_gate"], w["w1_up"], w["w1_down"], n)
    qt, k, _, ckv, kr, u, gb, mq = _premix(x[None], tables, w, n)
    q = qt[0].T
    a = _absorb(q, w["gk_slot"], w["wuk_t_pad"])
    qr = q.reshape(n, N_HEADS, HEAD_SLOT)[:, :, ROPE_LO:ROPE_LO + QK_ROPE].astype(F32)
    acc, m, l = _paged_mla(page_table, w["wuk_t2d"], jnp.transpose(a, (1, 0, 2)), qr,
                           cache_ckv, cache_krope, page_base)
    head_major = lambda t: jnp.transpose(t, (1, 0, 2))
    o_mem = _sample_mem_attend(mq[0].astype(F32), mem_k.reshape(n, -1, MEM_WIDTH), mem_v.reshape(n, -1, MEM_WIDTH),
                               w["gsel"], w["gsel_t"])
    x = _sample_postmix(x, q, k[0], ckv[0], head_major(acc), head_major(m), head_major(l),
                        w["wuv_bd"], u[0], gb[0], state[:, 0, :], state[:, 1, :], w["conv_w"], o_mem,
                        w["g_out_mla"], w["g_out_conv"], w["g_out_mem"], w["w_o"])
    x = _ffn_half(x, w["g_ffn2"], w["w2_gate"], w["w2_up"], w["w2_down"], n)
    return x, ckv[0], kr[0], jnp.stack([state[:, 1, :], u[0]], axis=1)


def kernel(x_prompt, mem_prompt, x_sample, cache_ckv, cache_krope, page_table, state_conv, cache_mem_k,
           cache_mem_v, g_ffn1, w1_gate, w1_up, w1_down, g_mix, w_in, g_q_lora, w_uq, g_qn, g_qr,
           g_kv_lora, w_uk, w_uv, g_kn, g_kr, conv_w, g_mem, w_mem_k, w_mem_v, g_mq, g_mk, g_out_mla,
           g_out_conv, g_out_mem, w_o, g_ffn2, w2_gate, w2_up, w2_down):
    params = dict(g_ffn1=g_ffn1, w1_gate=w1_gate, w1_up=w1_up, w1_down=w1_down, g_mix=g_mix, w_in=w_in,
                  g_q_lora=g_q_lora, w_uq=w_uq, g_qn=g_qn, g_qr=g_qr, g_kv_lora=g_kv_lora, w_uk=w_uk,
                  w_uv=w_uv, g_kn=g_kn, g_kr=g_kr, conv_w=conv_w, g_mem=g_mem, w_mem_k=w_mem_k,
                  w_mem_v=w_mem_v, g_mq=g_mq, g_mk=g_mk, g_out_mla=g_out_mla, g_out_conv=g_out_conv,
                  g_out_mem=g_out_mem, w_o=w_o, g_ffn2=g_ffn2, w2_gate=w2_gate, w2_up=w2_up,
                  w2_down=w2_down)
    depth = w_in.shape[0]
    b, s, _ = x_prompt.shape
    nb, dec_seq, _ = x_sample.shape
    assert dec_seq == 1
    n_phys = cache_ckv.shape[1]
    tm = min(512, s)
    tab_p = _rope_tables(jnp.arange(s))
    tab_s = _rope_tables(jnp.full((nb,), PAST_LEN, jnp.int32))
    ckv_pages = cache_ckv.reshape(depth * n_phys, PAGE_SIZE, KV_LORA)
    kr_pages = jnp.swapaxes(cache_krope, 2, 3).reshape(depth * n_phys, QK_ROPE, PAGE_SIZE)

    xp, xs = x_prompt, x_sample.reshape(nb, D_MODEL)
    outs_p, outs_s = [], []
    for l in range(depth):
        w = _layer_weights(l, params)
        xp, ckv, kr, u, mk, mv = _prompt_layer(xp, mem_prompt, w, tab_p, tm)
        outs_p.append((ckv, kr, u[:, -(CONV_K - 1):], mk.reshape(b, -1, MEM_HEADS, MEM_HEAD_DIM),
                       mv.reshape(b, -1, MEM_HEADS, MEM_HEAD_DIM)))
        xs, ckv_s, kr_s, conv_s = _sample_layer(xs, w, tab_s, ckv_pages, kr_pages, l * n_phys, page_table,
                                                state_conv[l], cache_mem_k[l], cache_mem_v[l])
        outs_s.append((ckv_s[:, None, :], kr_s[:, None, :], conv_s))
    stack = lambda items, i: jnp.stack([it[i] for it in items])
    return (xp, xs.reshape(nb, 1, D_MODEL), stack(outs_p, 0), stack(outs_p, 1), stack(outs_p, 2),
            stack(outs_p, 3), stack(outs_p, 4), stack(outs_s, 0), stack(outs_s, 1), stack(outs_s, 2))
```

```python
import functools

import jax
import jax.numpy as jnp
from jax import lax
from jax.experimental import pallas as pl
from jax.experimental.pallas import tpu as pltpu

F32 = jnp.float32
BF16 = jnp.bfloat16

D_MODEL = 1024
N_HEADS = 8
Q_LORA = 256
KV_LORA = 128
QK_NOPE = 64
QK_ROPE = 32
V_HEAD = 64
MLA_WIDTH = N_HEADS * V_HEAD
CONV_WIDTH = 256
CONV_K = 3
MEM_HEADS = 4
MEM_HEAD_DIM = 64
MEM_WIDTH = MEM_HEADS * MEM_HEAD_DIM
D_FF = 2816
ROPE_THETA = 10000.0
EPS = 1e-6
PAST_LEN = 16384
PAGE_SIZE = 128
MLA_SCALE = (QK_NOPE + QK_ROPE) ** -0.5
MEM_SCALE = MEM_HEAD_DIM ** -0.5
LOG2E = 1.4426950408889634

HEAD_SLOT = 128
ROPE_LO = QK_NOPE
HALF_ROPE = QK_ROPE // 2
Z_WIDTH = 1536
VT_ROWS = 80
NEG = -1e30
PAGES_PER_CHUNK = 32
PAGE_SLOTS = 3
KEY_SUB = 256


def _rms(x, g):
    ms = jnp.mean(x * x, axis=-1, keepdims=True)
    return x * lax.rsqrt(ms + EPS) * g


def _dot(a, b):
    return jnp.dot(a, b, preferred_element_type=F32)


def _dot_nt(a, b):
    return lax.dot_general(a, b, (((1,), (1,)), ((), ())), preferred_element_type=F32)


def _group_mean_sq(x, g_ref):
    x2 = (x * x).astype(BF16)
    g = g_ref[...]
    cols = [_dot(x2[:, j * 256:(j + 1) * 256], g) for j in range(x.shape[1] // 256)]
    return cols[0] if len(cols) == 1 else jnp.concatenate(cols, axis=1)


def _rope_slab(v, cos, s1, s2):
    return (v * cos + pltpu.roll(v, HEAD_SLOT - HALF_ROPE, 1) * s1
            + pltpu.roll(v, HALF_ROPE, 1) * s2)


def _const_spec(shape):
    nd = len(shape)
    return pl.BlockSpec(shape, lambda *_: (0,) * nd, pipeline_mode=pl.Buffered(1))


def _ffn_body(x_ref, g_ref, wg_ref, wu_ref, wd_ref, o_ref):
    x = x_ref[...]
    xn = _rms(x, g_ref[...]).astype(BF16)
    h = _dot(xn, wg_ref[...])
    u = _dot(xn, wu_ref[...])
    a = (h / (1.0 + jnp.exp(-h)) * u).astype(BF16)
    o_ref[...] = x + 0.5 * _dot(a, wd_ref[...])


def _ffn_half(x, g, wg, wu, wd, tm):
    n, d = x.shape
    dff = wg.shape[1]
    return pl.pallas_call(
        _ffn_body,
        grid=(n // tm,),
        in_specs=[pl.BlockSpec((tm, d), lambda i: (i, 0)),
                  _const_spec((1, d)), _const_spec((d, dff)), _const_spec((d, dff)),
                  _const_spec((dff, d))],
        out_specs=pl.BlockSpec((tm, d), lambda i: (i, 0)),
        out_shape=jax.ShapeDtypeStruct((n, d), F32),
        compiler_params=pltpu.CompilerParams(dimension_semantics=("parallel",)),
        name="ffn_half",
    )(x, g, wg, wu, wd)


def _premix_body(x_ref, cos_ref, s1_ref, s2_ref, gmix_ref, win_ref, gql_ref, wuq_ref, gq_ref,
                 g2_ref, gkvl_ref, gkr_ref, wuk_ref, gk_ref, wuv_ref, gmq_ref, g64_ref,
                 qt_ref, k_ref, vt_ref, ckv_ref, kr_ref, u_ref, gb_ref, mq_ref):
    tm = x_ref.shape[0]
    hn = _rms(x_ref[...], gmix_ref[...]).astype(BF16)
    z = _dot(hn, win_ref[...])
    c_q, c_kv, k_r = z[:, 0:256], z[:, 256:384], z[:, 384:512]
    u_in, g_b, g_c, m_q = z[:, 512:768], z[:, 768:1024], z[:, 1024:1280], z[:, 1280:1536]
    cos, s1, s2 = cos_ref[...], s1_ref[...], s2_ref[...]

    q = _dot(_rms(c_q, gql_ref[...]).astype(BF16), wuq_ref[...])
    q = q * lax.rsqrt(_group_mean_sq(q, g2_ref) + EPS) * gq_ref[...]
    q = jnp.concatenate(
        [_rope_slab(q[:, h * HEAD_SLOT:(h + 1) * HEAD_SLOT], cos, s1, s2) for h in range(N_HEADS)],
        axis=1)
    qt_ref[...] = q.T.astype(BF16)

    ckv = _rms(c_kv, gkvl_ref[...])
    ckv_ref[...] = ckv
    ckv16 = ckv.astype(BF16)
    kr_ms = jnp.sum(k_r * k_r, axis=-1, keepdims=True) * (1.0 / QK_ROPE)
    krr = _rope_slab(k_r * lax.rsqrt(kr_ms + EPS) * gkr_ref[...], cos, s1, s2)
    kr_ref[...] = krr[:, ROPE_LO:ROPE_LO + QK_ROPE]
    kk = _dot(ckv16, wuk_ref[...])
    kk = kk * lax.rsqrt(_group_mean_sq(kk, g2_ref) + EPS) * gk_ref[...]
    k_ref[...] = jnp.concatenate(
        [kk[:, h * HEAD_SLOT:(h + 1) * HEAD_SLOT] + krr for h in range(N_HEADS)],
        axis=1).astype(BF16)
    vt = _dot(ckv16, wuv_ref[...]).T
    tkv = vt_ref.shape[2]
    ones_rows = jnp.where(lax.broadcasted_iota(jnp.int32, (VT_ROWS - V_HEAD, tkv), 0) == 0,
                          1.0, 0.0).astype(BF16)
    for u in range(tm // tkv):
        for h in range(N_HEADS):
            vt_ref[u, h * VT_ROWS:h * VT_ROWS + V_HEAD, :] = (
                vt[h * V_HEAD:(h + 1) * V_HEAD, u * tkv:(u + 1) * tkv].astype(BF16))
            vt_ref[u, h * VT_ROWS + V_HEAD:(h + 1) * VT_ROWS, :] = ones_rows

    u_ref[...] = g_c * u_in
    gb_ref[...] = g_b
    mq = m_q * lax.rsqrt(_group_mean_sq(m_q, g64_ref) + EPS) * gmq_ref[...]
    mq_ref[...] = mq.astype(BF16)


def _premix(x, tables, w, tm):
    b, s, d = x.shape
    tkv = min(KEY_SUB, tm)
    cos, s1, s2 = tables
    row = lambda width: pl.BlockSpec((None, tm, width), lambda bi, i: (bi, i, 0))
    tab = pl.BlockSpec((tm, HEAD_SLOT), lambda bi, i: (i, 0))
    consts = [w["g_mix"], w["w_in"], w["g_q_lora"], w["w_uq"], w["gq_vec"], w["G2"], w["g_kv_lora"],
              w["gkr_vec"], w["w_uk"], w["gk_vec"], w["w_uv"], w["gmq_vec"], w["G64"]]
    out_shape = (
        jax.ShapeDtypeStruct((b, N_HEADS * HEAD_SLOT, s), BF16),
        jax.ShapeDtypeStruct((b, s, N_HEADS * HEAD_SLOT), BF16),
        jax.ShapeDtypeStruct((b, s // tkv, N_HEADS * VT_ROWS, tkv), BF16),
        jax.ShapeDtypeStruct((b, s, KV_LORA), F32),
        jax.ShapeDtypeStruct((b, s, QK_ROPE), F32),
        jax.ShapeDtypeStruct((b, s, CONV_WIDTH), F32),
        jax.ShapeDtypeStruct((b, s, CONV_WIDTH), F32),
        jax.ShapeDtypeStruct((b, s, MEM_WIDTH), BF16),
    )
    out_specs = (
        pl.BlockSpec((None, N_HEADS * HEAD_SLOT, tm), lambda bi, i: (bi, 0, i)),
        row(N_HEADS * HEAD_SLOT),
        pl.BlockSpec((None, tm // tkv, N_HEADS * VT_ROWS, tkv), lambda bi, i: (bi, i, 0, 0)),
        row(KV_LORA), row(QK_ROPE), row(CONV_WIDTH), row(CONV_WIDTH), row(MEM_WIDTH),
    )
    return pl.pallas_call(
        _premix_body,
        grid=(b, s // tm),
        in_specs=[row(d), tab, tab, tab] + [_const_spec(c.shape) for c in consts],
        out_specs=out_specs,
        out_shape=out_shape,
        compiler_params=pltpu.CompilerParams(dimension_semantics=("parallel", "parallel")),
        name="premix",
    )(x, cos, s1, s2, *consts)


def _attn_body(qt_ref, k_ref, vt_ref, o_ref, s_even, s_odd):
    tq = qt_ref.shape[1]
    tk = vt_ref.shape[2]
    assert tq == 2 * tk
    qi = pl.program_id(2)
    q_t = [qt_ref[j * HEAD_SLOT:(j + 1) * HEAD_SLOT, :] for j in range(2)]
    kpos = lax.broadcasted_iota(jnp.int32, (tk, tq), 0)
    qpos = lax.broadcasted_iota(jnp.int32, (tk, tq), 1)

    def scores(i, j, s_ref):
        rows = pl.ds(pl.multiple_of(i * tk, tk), tk)
        s_ref[j] = _dot(k_ref[rows, j * HEAD_SLOT:(j + 1) * HEAD_SLOT], q_t[j])

    def absorb(i, j, s_ref, state, diag_offset=None):
        m, acc = state[2 * j], state[2 * j + 1]
        s_t = s_ref[j]
        if diag_offset is not None:
            s_t = jnp.where(kpos + diag_offset <= qpos, s_t, NEG)
        m_new = jnp.maximum(m, jnp.max(s_t, axis=0, keepdims=True))
        alpha = jnp.exp2(m - m_new)
        p = jnp.exp2(s_t - m_new).astype(BF16)
        v = vt_ref[i, j * VT_ROWS:(j + 1) * VT_ROWS, :]
        state = list(state)
        state[2 * j], state[2 * j + 1] = m_new, alpha * acc + _dot(v, p)
        return tuple(state)

    def pair(g, state):
        scores(2 * g + 1, 1, s_odd)
        state = absorb(2 * g, 0, s_even, state)
        scores(2 * g + 2, 0, s_even)
        state = absorb(2 * g, 1, s_even, state)
        scores(2 * g + 2, 1, s_even)
        state = absorb(2 * g + 1, 0, s_odd, state)
        scores(2 * g + 3, 0, s_odd)
        return absorb(2 * g + 1, 1, s_odd, state)

    scores(0, 0, s_even)
    scores(0, 1, s_even)
    scores(1, 0, s_odd)
    init = (jnp.full((1, tq), NEG, F32), jnp.zeros((VT_ROWS, tq), F32)) * 2
    n_double = qi // 2
    state = lax.fori_loop(0, n_double, lambda g, st: pair(2 * g + 1, pair(2 * g, st)), init)
    state = lax.fori_loop(2 * n_double, qi, pair, state)
    scores(2 * qi + 1, 1, s_odd)
    for j in range(2):
        state = absorb(2 * qi, j, s_even, state, 0)
    for j in range(2):
        state = absorb(2 * qi + 1, j, s_odd, state, tk)
    outs = [state[2 * j + 1][0:V_HEAD] / state[2 * j + 1][V_HEAD:V_HEAD + 1] for j in range(2)]
    o_ref[...] = jnp.concatenate(outs, axis=0).T


def _mla_prompt(qt, k, vt, tq):
    b, _, s = qt.shape
    nkt, tk = vt.shape[1], vt.shape[3]
    return pl.pallas_call(
        _attn_body,
        grid=(b, N_HEADS // 2, s // tq),
        in_specs=[pl.BlockSpec((None, 2 * HEAD_SLOT, tq), lambda bi, p, qi: (bi, p, qi)),
                  pl.BlockSpec((None, s, 2 * HEAD_SLOT), lambda bi, p, qi: (bi, 0, p)),
                  pl.BlockSpec((None, nkt, 2 * VT_ROWS, tk), lambda bi, p, qi: (bi, 0, p, 0))],
        out_specs=pl.BlockSpec((None, tq, 2 * V_HEAD), lambda bi, p, qi: (bi, qi, p)),
        out_shape=jax.ShapeDtypeStruct((b, s, MLA_WIDTH), F32),
        scratch_shapes=[pltpu.VMEM((2, tk, tq), F32), pltpu.VMEM((2, tk, tq), F32)],
        compiler_params=pltpu.CompilerParams(
            dimension_semantics=("parallel", "parallel", "arbitrary")),
        name="mla_prompt",
    )(qt, k, vt)


def _memkv_body(mem_ref, g_ref, wk_ref, wv_ref, g64_ref, gmk_ref, k_ref, v_ref):
    hm = _rms(mem_ref[...], g_ref[...]).astype(BF16)
    k = _dot(hm, wk_ref[...])
    k_ref[...] = k * lax.rsqrt(_group_mean_sq(k, g64_ref) + EPS) * gmk_ref[...]
    v_ref[...] = _dot(hm, wv_ref[...])


def _mem_kv(mem, w):
    b, n, d = mem.shape
    consts = [w["g_mem"], w["w_mem_k"], w["w_mem_v"], w["G64"], w["gmk_vec"]]
    blk = pl.BlockSpec((None, n, MEM_WIDTH), lambda bi: (bi, 0, 0))
    return pl.pallas_call(
        _memkv_body,
        grid=(b,),
        in_specs=[pl.BlockSpec((None, n, d), lambda bi: (bi, 0, 0))]
        + [_const_spec(c.shape) for c in consts],
        out_specs=(blk, blk),
        out_shape=(jax.ShapeDtypeStruct((b, n, MEM_WIDTH), F32),) * 2,
        compiler_params=pltpu.CompilerParams(dimension_semantics=("parallel",)),
        name="mem_kv",
    )(mem, *consts)


def _out_proj(x, o_mla, y_conv, o_mem, g1_ref, g2_ref, g3_ref, wo_ref):
    o = _dot(_rms(o_mla, g1_ref[...]).astype(BF16), wo_ref[0:MLA_WIDTH, :])
    o += _dot(_rms(y_conv, g2_ref[...]).astype(BF16), wo_ref[MLA_WIDTH:MLA_WIDTH + CONV_WIDTH, :])
    o += _dot(_rms(o_mem, g3_ref[...]).astype(BF16), wo_ref[MLA_WIDTH + CONV_WIDTH:, :])
    return x + o


def _postmix_body(x_ref, omla_ref, u_ref, uprev_ref, gb_ref, mq_ref, mk_ref, mv_ref, cw_ref,
                  g1_ref, g2_ref, g3_ref, wo_ref, o_ref, ubuf):
    tm = x_ref.shape[0]
    n_mem = mv_ref.shape[0] // MEM_HEADS
    u = u_ref[...]
    ubuf[0:8, :] = jnp.where(pl.program_id(1) == 0, 0.0, uprev_ref[...])
    ubuf[8:8 + tm, :] = u
    cw = cw_ref[...]
    y = cw[0:1] * ubuf[pl.ds(6, tm), :] + cw[1:2] * ubuf[pl.ds(7, tm), :] + cw[2:3] * u
    y = gb_ref[...] * y
    s = _dot(mq_ref[...], mk_ref[...])
    ps = []
    for h in range(MEM_HEADS):
        sh = s[:, h * n_mem:(h + 1) * n_mem]
        p = jnp.exp2(sh - jnp.max(sh, axis=-1, keepdims=True))
        ps.append((p / jnp.sum(p, axis=-1, keepdims=True)).astype(BF16))
    o_mem = _dot(jnp.concatenate(ps, axis=1), mv_ref[...])
    o_ref[...] = _out_proj(x_ref[...], omla_ref[...], y, o_mem, g1_ref, g2_ref, g3_ref, wo_ref)


def _postmix(x, o_mla, u, gb, mq, mk_bd, mv_bd, w, tm):
    b, s, d = x.shape
    row = lambda width: pl.BlockSpec((None, tm, width), lambda bi, i: (bi, i, 0))
    prev = pl.BlockSpec((None, 8, CONV_WIDTH), lambda bi, i: (bi, jnp.maximum(i * (tm // 8) - 1, 0), 0))
    per_b = lambda a: pl.BlockSpec((None,) + a.shape[1:], lambda bi, i: (bi, 0, 0))
    consts = [w["conv_w"], w["g_out_mla"], w["g_out_conv"], w["g_out_mem"], w["w_o"]]
    return pl.pallas_call(
        _postmix_body,
        grid=(b, s // tm),
        in_specs=[row(d), row(MLA_WIDTH), row(CONV_WIDTH), prev, row(CONV_WIDTH), row(MEM_WIDTH),
                  per_b(mk_bd), per_b(mv_bd)] + [_const_spec(c.shape) for c in consts],
        out_specs=row(d),
        out_shape=jax.ShapeDtypeStruct((b, s, d), F32),
        scratch_shapes=[pltpu.VMEM((tm + 8, CONV_WIDTH), F32)],
        compiler_params=pltpu.CompilerParams(dimension_semantics=("parallel", "parallel")),
        name="postmix",
    )(x, o_mla, u, u, gb, mq, mk_bd, mv_bd, *consts)


def _absorb_body(q_ref, gk_ref, wt_ref, a_ref):
    for h in range(N_HEADS):
        qh = q_ref[:, h * HEAD_SLOT:(h + 1) * HEAD_SLOT].astype(F32) * gk_ref[...]
        a_ref[h] = _dot(qh.astype(BF16), wt_ref[h])


def _absorb(q, gk_slot, wuk_t):
    n = q.shape[0]
    return pl.pallas_call(
        _absorb_body,
        out_shape=jax.ShapeDtypeStruct((N_HEADS, n, KV_LORA), F32),
        name="absorb_q",
    )(q, gk_slot, wuk_t)


def _paged_body(pt_ref, wt_ref, a_ref, qr_ref, ckv_hbm, kr_hbm, acc_ref, m_ref, l_ref,
                xs, krs, lhs, s_scr, sem, *, n_chunks, ppc, page_base):
    nk = ppc * PAGE_SIZE
    nope = N_HEADS * QK_NOPE
    b = pl.program_id(0)
    last = pl.num_programs(0) * n_chunks - 1

    def copies(t, slot):
        out = []
        for i in range(ppc):
            pg = pt_ref[t * ppc + i] + page_base
            keys = pl.ds(i * PAGE_SIZE, PAGE_SIZE)
            out.append(pltpu.make_async_copy(ckv_hbm.at[pg], xs.at[slot, keys, :], sem.at[slot, 0]))
            out.append(pltpu.make_async_copy(kr_hbm.at[pg], krs.at[slot, :, keys], sem.at[slot, 1]))
        return out

    def start(t, slot):
        for cp in copies(t, slot):
            cp.start()

    def wait(t, slot):
        for cp in copies(t, slot):
            cp.wait()

    def scores(t, slot):
        e = t // n_chunks
        wait(t, slot)
        lhs[nope:, :] = jnp.concatenate([a_ref[e], jnp.zeros((8, KV_LORA), F32)], axis=0).astype(BF16)
        qr = jnp.concatenate([qr_ref[e], jnp.zeros((8, QK_ROPE), F32)], axis=0).astype(BF16)
        out = _dot_nt(lhs[...], xs[slot].astype(BF16))
        k_t = out[0:nope]
        n = jnp.sum((k_t * k_t).reshape(N_HEADS, QK_NOPE, nk), axis=1)
        rope = _dot(qr, krs[slot].astype(BF16))[0:N_HEADS]
        return out[nope:nope + N_HEADS] * lax.rsqrt(n * (1.0 / QK_NOPE) + EPS) + rope

    @pl.when(b == 0)
    def _():
        lhs[0:nope, :] = wt_ref[...]
        start(0, 0)
        start(1, 1)
        s_scr[...] = scores(0, 0)

    def step(c, carry):
        m, l, acc, s = carry
        t = b * n_chunks + c
        start(jnp.minimum(t + 2, last), (t + 2) % PAGE_SLOTS)
        s_next = scores(jnp.minimum(t + 1, last), (t + 1) % PAGE_SLOTS)
        m_new = jnp.maximum(m, jnp.max(s, axis=-1, keepdims=True))
        alpha = jnp.exp2(m - m_new)
        p = jnp.exp2(s - m_new)
        l = alpha * l + jnp.sum(p, axis=-1, keepdims=True)
        p16 = jnp.concatenate([p, jnp.zeros_like(p)], axis=0).astype(BF16)
        acc = alpha * acc + _dot(p16, xs[t % PAGE_SLOTS].astype(BF16))[0:N_HEADS]
        return m_new, l, acc, s_next

    init = (jnp.full((N_HEADS, 1), NEG, F32), jnp.zeros((N_HEADS, 1), F32),
            jnp.zeros((N_HEADS, KV_LORA), F32), s_scr[...])
    m, l, acc, s = lax.fori_loop(0, n_chunks, step, init)
    s_scr[...] = s
    acc_ref[0] = acc
    m_ref[0] = jnp.broadcast_to(m, (N_HEADS, HEAD_SLOT))
    l_ref[0] = jnp.broadcast_to(l, (N_HEADS, HEAD_SLOT))

    @pl.when(b == pl.num_programs(0) - 1)
    def _():
        wait(last, (last + 2) % PAGE_SLOTS)


def _paged_mla(page_table, wuk_t2d, a, qr, cache_ckv, cache_krope, page_base):
    nb, n_pages = page_table.shape
    ppc = min(PAGES_PER_CHUNK, n_pages)
    n_chunks = n_pages // ppc
    assert n_chunks * ppc == n_pages and nb * n_chunks >= 2
    nk = ppc * PAGE_SIZE
    vmem = pl.BlockSpec(memory_space=pltpu.VMEM)
    per_b = pl.BlockSpec((1, N_HEADS, HEAD_SLOT), lambda bi: (bi, 0, 0))
    out = jax.ShapeDtypeStruct((nb, N_HEADS, HEAD_SLOT), F32)
    return pl.pallas_call(
        functools.partial(_paged_body, n_chunks=n_chunks, ppc=ppc, page_base=page_base),
        grid=(nb,),
        in_specs=[pl.BlockSpec(memory_space=pltpu.SMEM), vmem, vmem, vmem,
                  pl.BlockSpec(memory_space=pl.ANY), pl.BlockSpec(memory_space=pl.ANY)],
        out_specs=(per_b, per_b, per_b),
        out_shape=(out, out, out),
        scratch_shapes=[pltpu.VMEM((PAGE_SLOTS, nk, KV_LORA), F32),
                        pltpu.VMEM((PAGE_SLOTS, QK_ROPE, nk), F32),
                        pltpu.VMEM((N_HEADS * QK_NOPE + 16, KV_LORA), BF16),
                        pltpu.VMEM((N_HEADS, nk), F32),
                        pltpu.SemaphoreType.DMA((PAGE_SLOTS, 2))],
        compiler_params=pltpu.CompilerParams(dimension_semantics=("arbitrary",)),
        name="paged_mla",
    )(page_table.reshape(-1), wuk_t2d, a, qr, cache_ckv, cache_krope)


def _smem_body(mq_ref, kt_ref, vt_ref, o_ref):
    bc, w, n_mem = kt_ref.shape
    step = pl.program_id(0)
    lane = lax.broadcasted_iota(jnp.int32, o_ref.shape, 1)

    @pl.when(step == 0)
    def _():
        o_ref[...] = jnp.zeros_like(o_ref)

    out = o_ref[...]
    for i in range(bc):
        q = jnp.concatenate([mq_ref[i]] * (n_mem // HEAD_SLOT), axis=1)
        s = jnp.sum((kt_ref[i] * q).reshape(MEM_HEADS, MEM_HEAD_DIM, n_mem), axis=1)
        p = jnp.exp2(s - jnp.max(s, axis=-1, keepdims=True))
        p = p / jnp.sum(p, axis=-1, keepdims=True)
        col = jnp.concatenate(
            [jnp.sum(vt_ref[i, h * MEM_HEAD_DIM:(h + 1) * MEM_HEAD_DIM, :] * p[h:h + 1, :],
                     axis=-1, keepdims=True) for h in range(MEM_HEADS)], axis=0)
        out = jnp.where(lane == step * bc + i, col, out)
    o_ref[...] = out


def _sample_mem_attend(mq_lanes, kt, vt, bc=8):
    nb, w, n_mem = kt.shape
    return pl.pallas_call(
        _smem_body,
        grid=(nb // bc,),
        in_specs=[pl.BlockSpec((bc, w, HEAD_SLOT), lambda i: (i, 0, 0)),
                  pl.BlockSpec((bc, w, n_mem), lambda i: (i, 0, 0)),
                  pl.BlockSpec((bc, w, n_mem), lambda i: (i, 0, 0))],
        out_specs=pl.BlockSpec((w, nb), lambda i: (0, 0)),
        out_shape=jax.ShapeDtypeStruct((w, nb), F32),
        compiler_params=pltpu.CompilerParams(dimension_semantics=("arbitrary",)),
        name="sample_mem_attend",
    )(mq_lanes, kt, vt)


def _spost_body(x_ref, q_ref, k_ref, ckv_ref, acc_ref, m_ref, l_ref, wuv_ref, u_ref, gb_ref,
                s0_ref, s1_ref, cw_ref, omem_ref, g1_ref, g2_ref, g3_ref, wo_ref, o_ref):
    ckv = ckv_ref[...]
    o_mla = jnp.zeros((x_ref.shape[0], MLA_WIDTH), F32)
    for h in range(N_HEADS):
        sl = slice(h * HEAD_SLOT, (h + 1) * HEAD_SLOT)
        s_new = jnp.sum(q_ref[:, sl].astype(F32) * k_ref[:, sl].astype(F32), axis=-1, keepdims=True)
        m_old = m_ref[h][:, 0:1]
        l_old = l_ref[h][:, 0:1]
        m_new = jnp.maximum(m_old, s_new)
        alpha = jnp.exp2(m_old - m_new)
        p_new = jnp.exp2(s_new - m_new)
        o_lat = (acc_ref[h] * alpha + p_new * ckv) / (l_old * alpha + p_new)
        o_mla += _dot(o_lat.astype(BF16), wuv_ref[h])
    cw = cw_ref[...]
    y = gb_ref[...] * (cw[0:1] * s0_ref[...] + cw[1:2] * s1_ref[...] + cw[2:3] * u_ref[...])
    o_ref[...] = _out_proj(x_ref[...], o_mla, y, omem_ref[...], g1_ref, g2_ref, g3_ref, wo_ref)


def _sample_postmix(*args):
    n = args[0].shape[0]
    return pl.pallas_call(
        _spost_body,
        out_shape=jax.ShapeDtypeStruct((n, D_MODEL), F32),
        name="sample_postmix",
    )(*args)


def _rope_tables(pos):
    inv_freq = ROPE_THETA ** (-jnp.arange(0, QK_ROPE, 2, dtype=F32) / QK_ROPE)
    ang = pos.astype(F32)[:, None] * inv_freq[None, :]
    cos, sin = jnp.cos(ang), jnp.sin(ang)
    n = pos.shape[0]
    zeros = lambda w_: jnp.zeros((n, w_), F32)
    tail = HEAD_SLOT - ROPE_LO - QK_ROPE
    c = jnp.concatenate([jnp.ones((n, ROPE_LO), F32), cos, cos, zeros(tail)], axis=1)
    s1 = jnp.concatenate([zeros(ROPE_LO), -sin, zeros(HALF_ROPE), zeros(tail)], axis=1)
    s2 = jnp.concatenate([zeros(ROPE_LO), zeros(HALF_ROPE), sin, zeros(tail)], axis=1)
    return c, s1, s2


def _block_diag_avg(sizes, width):
    idx = jnp.arange(width)
    gid = jnp.full((width,), -1, jnp.int32)
    scale = jnp.zeros((width,), F32)
    lo = 0
    for g, sz in enumerate(sizes):
        inside = (idx >= lo) & (idx < lo + sz)
        gid = jnp.where(inside, g, gid)
        scale = jnp.where(inside, 1.0 / sz, scale)
        lo += sz
    same = (gid[:, None] == gid[None, :]) & (gid[:, None] >= 0)
    return jnp.where(same, scale[None, :], 0.0).astype(BF16)


def _layer_weights(l, p):
    pad_last = lambda a, n: jnp.pad(a, [(0, 0)] * (a.ndim - 1) + [(0, n - a.shape[-1])])
    row = lambda v: v.reshape(1, -1).astype(F32)
    w_in = p["w_in"][l]
    zc = lambda n: jnp.zeros((D_MODEL, n), F32)
    w_in_p = jnp.concatenate([w_in[:, 0:384], zc(ROPE_LO), w_in[:, 384:416],
                              zc(HEAD_SLOT - ROPE_LO - QK_ROPE), w_in[:, 416:]], axis=1)
    slot_gain = lambda nope, rope: jnp.concatenate(
        [nope, rope, jnp.zeros((HEAD_SLOT - QK_NOPE - QK_ROPE,), F32)])
    zeros_n = jnp.zeros((QK_NOPE,), F32)
    zeros_r = jnp.zeros((QK_ROPE,), F32)
    g_kn = p["g_kn"][l]
    w_uk = p["w_uk"][l]
    g_slot = _block_diag_avg((QK_NOPE, QK_ROPE), HEAD_SLOT).astype(F32)
    g2 = jnp.kron(jnp.eye(2, dtype=F32), g_slot).astype(BF16)
    wuv = p["w_uv"][l]
    wuv_bd = jnp.stack([jnp.pad(wuv[:, h, :], ((0, 0), (h * V_HEAD, MLA_WIDTH - (h + 1) * V_HEAD)))
                        for h in range(N_HEADS)])
    return {
        "g_ffn1": row(p["g_ffn1"][l]), "w1_gate": p["w1_gate"][l].astype(BF16),
        "w1_up": p["w1_up"][l].astype(BF16), "w1_down": p["w1_down"][l].astype(BF16),
        "g_ffn2": row(p["g_ffn2"][l]), "w2_gate": p["w2_gate"][l].astype(BF16),
        "w2_up": p["w2_up"][l].astype(BF16), "w2_down": p["w2_down"][l].astype(BF16),
        "g_mix": row(p["g_mix"][l]), "w_in": w_in_p.astype(BF16),
        "g_q_lora": row(p["g_q_lora"][l]),
        "w_uq": pad_last(p["w_uq"][l].reshape(Q_LORA, N_HEADS, QK_NOPE + QK_ROPE), HEAD_SLOT)
        .reshape(Q_LORA, N_HEADS * HEAD_SLOT).astype(BF16),
        "gq_vec": row(jnp.tile(slot_gain(p["g_qn"][l], p["g_qr"][l]), N_HEADS) * (MLA_SCALE * LOG2E)),
        "G2": g2,
        "g_kv_lora": row(p["g_kv_lora"][l]),
        "gkr_vec": row(slot_gain(zeros_n, p["g_kr"][l])),
        "w_uk": pad_last(w_uk, HEAD_SLOT).reshape(KV_LORA, N_HEADS * HEAD_SLOT).astype(BF16),
        "gk_vec": row(jnp.tile(slot_gain(g_kn, zeros_r), N_HEADS)),
        "gk_slot": row(slot_gain(g_kn, zeros_r)),
        "w_uv": wuv.reshape(KV_LORA, MLA_WIDTH).astype(BF16),
        "wuv_bd": wuv_bd.astype(BF16),
        "wuk_t_pad": jnp.pad(jnp.transpose(w_uk, (1, 2, 0)), ((0, 0), (0, HEAD_SLOT - QK_NOPE), (0, 0)))
        .astype(BF16),
        "wuk_t2d": jnp.transpose(w_uk, (1, 2, 0)).reshape(N_HEADS * QK_NOPE, KV_LORA).astype(BF16),
        "gmq_vec": row(jnp.tile(p["g_mq"][l], MEM_HEADS) * (MEM_SCALE * LOG2E)),
        "G64": _block_diag_avg((MEM_HEAD_DIM,) * MEM_HEADS, MEM_WIDTH),
        "g_mem": row(p["g_mem"][l]), "w_mem_k": p["w_mem_k"][l].astype(BF16),
        "w_mem_v": p["w_mem_v"][l].astype(BF16),
        "gmk_vec": row(jnp.tile(p["g_mk"][l], MEM_HEADS)),
        "conv_w": p["conv_w"][l].astype(F32),
        "g_out_mla": row(p["g_out_mla"][l]), "g_out_conv": row(p["g_out_conv"][l]),
        "g_out_mem": row(p["g_out_mem"][l]), "w_o": p["w_o"][l].astype(BF16),
    }


def _mem_block_diag(mk, mv):
    b, n, _ = mk.shape
    k4 = mk.reshape(b, n, MEM_HEADS, MEM_HEAD_DIM)
    v4 = mv.reshape(b, n, MEM_HEADS, MEM_HEAD_DIM)
    eye = jnp.eye(MEM_HEADS, dtype=F32)
    k_bd = jnp.einsum("bnhd,hg->bhdgn", k4, eye).reshape(b, MEM_WIDTH, MEM_HEADS * n)
    v_bd = jnp.einsum("bnhd,hg->bhngd", v4, eye).reshape(b, MEM_HEADS * n, MEM_WIDTH)
    return k_bd.astype(BF16), v_bd.astype(BF16)


def _prompt_layer(x, mem, w, tables, tm):
    b, s, d = x.shape
    x = _ffn_half(x.reshape(b * s, d), w["g_ffn1"], w["w1_gate"], w["w1_up"], w["w1_down"],
                  tm).reshape(b, s, d)
    qt, k, vt, ckv, kr, u, gb, mq = _premix(x, tables, w, tm)
    o_mla = _mla_prompt(qt, k, vt, 2 * vt.shape[3])
    mk, mv = _mem_kv(mem, w)
    mk_bd, mv_bd = _mem_block_diag(mk, mv)
    x = _postmix(x, o_mla, u, gb, mq, mk_bd, mv_bd, w, tm)
    x = _ffn_half(x.reshape(b * s, d), w["g_ffn2"], w["w2_gate"], w["w2_up"], w["w2_down"],
                  tm).reshape(b, s, d)
    return x, ckv, kr, u, mk, mv


def _sample_layer(x, w, tables, cache_ckv, cache_krope, page_base, page_table, state, mem_k, mem_v):
    n, d = x.shape
    x = _ffn_half(x, w["g_ffn1"], w["w1_gate"], w["w1_up"], w["w1_down"], n)
    qt, k, _, ckv, kr, u, gb, mq = _premix(x[None], tables, w, n)
    q = qt[0].T
    a = _absorb(q, w["gk_slot"], w["wuk_t_pad"])
    qr = q.reshape(n, N_HEADS, HEAD_SLOT)[:, :, ROPE_LO:ROPE_LO + QK_ROPE].astype(F32)
    acc, m, l = _paged_mla(page_table, w["wuk_t2d"], jnp.transpose(a, (1, 0, 2)), qr,
                           cache_ckv, cache_krope, page_base)
    head_major = lambda t: jnp.transpose(t, (1, 0, 2))
    feature_major = lambda t: jnp.transpose(t, (0, 2, 3, 1)).reshape(n, MEM_WIDTH, -1)
    mq_lanes = jnp.broadcast_to(mq[0].astype(F32)[:, :, None], (n, MEM_WIDTH, HEAD_SLOT))
    o_mem = _sample_mem_attend(mq_lanes, feature_major(mem_k), feature_major(mem_v)).T
    x = _sample_postmix(x, q, k[0], ckv[0], head_major(acc), head_major(m), head_major(l),
                        w["wuv_bd"], u[0], gb[0], state[:, 0, :], state[:, 1, :], w["conv_w"], o_mem,
                        w["g_out_mla"], w["g_out_conv"], w["g_out_mem"], w["w_o"])
    x = _ffn_half(x, w["g_ffn2"], w["w2_gate"], w["w2_up"], w["w2_down"], n)
    return x, ckv[0], kr[0], jnp.stack([state[:, 1, :], u[0]], axis=1)


def kernel(x_prompt, mem_prompt, x_sample, cache_ckv, cache_krope, page_table, state_conv, cache_mem_k,
           cache_mem_v, g_ffn1, w1_gate, w1_up, w1_down, g_mix, w_in, g_q_lora, w_uq, g_qn, g_qr,
           g_kv_lora, w_uk, w_uv, g_kn, g_kr, conv_w, g_mem, w_mem_k, w_mem_v, g_mq, g_mk, g_out_mla,
           g_out_conv, g_out_mem, w_o, g_ffn2, w2_gate, w2_up, w2_down):
    params = dict(g_ffn1=g_ffn1, w1_gate=w1_gate, w1_up=w1_up, w1_down=w1_down, g_mix=g_mix, w_in=w_in,
                  g_q_lora=g_q_lora, w_uq=w_uq, g_qn=g_qn, g_qr=g_qr, g_kv_lora=g_kv_lora, w_uk=w_uk,
                  w_uv=w_uv, g_kn=g_kn, g_kr=g_kr, conv_w=conv_w, g_mem=g_mem, w_mem_k=w_mem_k,
                  w_mem_v=w_mem_v, g_mq=g_mq, g_mk=g_mk, g_out_mla=g_out_mla, g_out_conv=g_out_conv,
                  g_out_mem=g_out_mem, w_o=w_o, g_ffn2=g_ffn2, w2_gate=w2_gate, w2_up=w2_up,
                  w2_down=w2_down)
    depth = w_in.shape[0]
    b, s, _ = x_prompt.shape
    nb, dec_seq, _ = x_sample.shape
    assert dec_seq == 1
    n_phys = cache_ckv.shape[1]
    tm = min(512, s)
    tab_p = _rope_tables(jnp.arange(s))
    tab_s = _rope_tables(jnp.full((nb,), PAST_LEN, jnp.int32))
    ckv_pages = cache_ckv.reshape(depth * n_phys, PAGE_SIZE, KV_LORA)
    kr_pages = jnp.swapaxes(cache_krope, 2, 3).reshape(depth * n_phys, QK_ROPE, PAGE_SIZE)

    xp, xs = x_prompt, x_sample.reshape(nb, D_MODEL)
    outs_p, outs_s = [], []
    for l in range(depth):
        w = _layer_weights(l, params)
        xp, ckv, kr, u, mk, mv = _prompt_layer(xp, mem_prompt, w, tab_p, tm)
        outs_p.append((ckv, kr, u[:, -(CONV_K - 1):], mk.reshape(b, -1, MEM_HEADS, MEM_HEAD_DIM),
                       mv.reshape(b, -1, MEM_HEADS, MEM_HEAD_DIM)))
        xs, ckv_s, kr_s, conv_s = _sample_layer(xs, w, tab_s, ckv_pages, kr_pages, l * n_phys, page_table,
                                                state_conv[l], cache_mem_k[l], cache_mem_v[l])
        outs_s.append((ckv_s[:, None, :], kr_s[:, None, :], conv_s))
    stack = lambda items, i: jnp.stack([it[i] for it in items])
    return (xp, xs.reshape(nb, 1, D_MODEL), stack(outs_p, 0), stack(outs_p, 1), stack(outs_p, 2),
            stack(outs_p, 3), stack(outs_p, 4), stack(outs_s, 0), stack(outs_s, 1), stack(outs_s, 2))
```

```python
import functools

import jax
import jax.numpy as jnp
from jax import lax
from jax.experimental import pallas as pl
from jax.experimental.pallas import tpu as pltpu

F32 = jnp.float32
BF16 = jnp.bfloat16

D_MODEL = 1024
N_HEADS = 8
Q_LORA = 256
KV_LORA = 128
QK_NOPE = 64
QK_ROPE = 32
V_HEAD = 64
MLA_WIDTH = N_HEADS * V_HEAD
CONV_WIDTH = 256
CONV_K = 3
MEM_HEADS = 4
MEM_HEAD_DIM = 64
MEM_WIDTH = MEM_HEADS * MEM_HEAD_DIM
D_FF = 2816
ROPE_THETA = 10000.0
EPS = 1e-6
PAST_LEN = 16384
PAGE_SIZE = 128
MLA_SCALE = (QK_NOPE + QK_ROPE) ** -0.5
MEM_SCALE = MEM_HEAD_DIM ** -0.5
LOG2E = 1.4426950408889634

HEAD_SLOT = 128
ROPE_LO = QK_NOPE
HALF_ROPE = QK_ROPE // 2
Z_WIDTH = 1536
VT_ROWS = 80
NEG = -1e30
PAGES_PER_CHUNK = 64
PAGE_SLOTS = 3
ROW_TILE = 512
KEY_SUB = 256
QUERY_TILE = 1024


def _rms(x, g):
    ms = jnp.mean(x * x, axis=-1, keepdims=True)
    return x * lax.rsqrt(ms + EPS) * g


def _dot(a, b):
    return jnp.dot(a, b, preferred_element_type=F32)


def _dot_nt(a, b):
    return lax.dot_general(a, b, (((1,), (1,)), ((), ())), preferred_element_type=F32)


def _group_mean_sq(x, g_ref):
    x2 = (x * x).astype(BF16)
    g = g_ref[...]
    cols = [_dot(x2[:, j * 256:(j + 1) * 256], g) for j in range(x.shape[1] // 256)]
    return cols[0] if len(cols) == 1 else jnp.concatenate(cols, axis=1)


def _rope_slab(v, cos, s1, s2):
    return (v * cos + pltpu.roll(v, HEAD_SLOT - HALF_ROPE, 1) * s1
            + pltpu.roll(v, HALF_ROPE, 1) * s2)


def _const_spec(shape):
    nd = len(shape)
    return pl.BlockSpec(shape, lambda *_: (0,) * nd, pipeline_mode=pl.Buffered(1))


def _swiglu_half_step(x, g_ref, wg_ref, wu_ref, wd_ref):
    xn = _rms(x, g_ref[...]).astype(BF16)
    h = _dot(xn, wg_ref[...])
    u = _dot(xn, wu_ref[...])
    a = (h / (1.0 + jnp.exp(-h)) * u).astype(BF16)
    return x + 0.5 * _dot(a, wd_ref[...])


def _ffn_body(x_ref, g_ref, wg_ref, wu_ref, wd_ref, o_ref):
    o_ref[...] = _swiglu_half_step(x_ref[...], g_ref, wg_ref, wu_ref, wd_ref)


def _ffn_half(x, g, wg, wu, wd, tm):
    n, d = x.shape
    dff = wg.shape[1]
    return pl.pallas_call(
        _ffn_body,
        grid=(n // tm,),
        in_specs=[pl.BlockSpec((tm, d), lambda i: (i, 0)),
                  _const_spec((1, d)), _const_spec((d, dff)), _const_spec((d, dff)),
                  _const_spec((dff, d))],
        out_specs=pl.BlockSpec((tm, d), lambda i: (i, 0)),
        out_shape=jax.ShapeDtypeStruct((n, d), F32),
        compiler_params=pltpu.CompilerParams(dimension_semantics=("parallel",)),
        name="ffn_half",
    )(x, g, wg, wu, wd)


def _premix_body(x_ref, cos_ref, s1_ref, s2_ref, gmix_ref, win_ref, gql_ref, wuq_ref, gq_ref,
                 g2_ref, gkvl_ref, gkr_ref, wuk_ref, gk_ref, wuv_ref, gmq_ref, g64_ref,
                 qt_ref, k_ref, vt_ref, ckv_ref, kr_ref, u_ref, gb_ref, mq_ref):
    tm = x_ref.shape[0]
    hn = _rms(x_ref[...], gmix_ref[...]).astype(BF16)
    z = _dot(hn, win_ref[...])
    c_q, c_kv, k_r = z[:, 0:256], z[:, 256:384], z[:, 384:512]
    u_in, g_b, g_c, m_q = z[:, 512:768], z[:, 768:1024], z[:, 1024:1280], z[:, 1280:1536]
    cos, s1, s2 = cos_ref[...], s1_ref[...], s2_ref[...]

    q = _dot(_rms(c_q, gql_ref[...]).astype(BF16), wuq_ref[...])
    q = q * lax.rsqrt(_group_mean_sq(q, g2_ref) + EPS) * gq_ref[...]
    q = jnp.concatenate(
        [_rope_slab(q[:, h * HEAD_SLOT:(h + 1) * HEAD_SLOT], cos, s1, s2) for h in range(N_HEADS)],
        axis=1)
    qt_ref[...] = q.T.astype(BF16)

    ckv = _rms(c_kv, gkvl_ref[...])
    ckv_ref[...] = ckv
    ckv16 = ckv.astype(BF16)
    kr_ms = jnp.sum(k_r * k_r, axis=-1, keepdims=True) * (1.0 / QK_ROPE)
    krr = _rope_slab(k_r * lax.rsqrt(kr_ms + EPS) * gkr_ref[...], cos, s1, s2)
    kr_ref[...] = krr.T[ROPE_LO:ROPE_LO + QK_ROPE, :]
    kk = _dot(ckv16, wuk_ref[...])
    kk = kk * lax.rsqrt(_group_mean_sq(kk, g2_ref) + EPS) * gk_ref[...]
    k_ref[...] = jnp.concatenate(
        [kk[:, h * HEAD_SLOT:(h + 1) * HEAD_SLOT] + krr for h in range(N_HEADS)],
        axis=1).astype(BF16)
    vt = _dot(ckv16, wuv_ref[...]).T
    tkv = vt_ref.shape[2]
    ones_rows = jnp.where(lax.broadcasted_iota(jnp.int32, (VT_ROWS - V_HEAD, tkv), 0) == 0,
                          1.0, 0.0).astype(BF16)
    for u in range(tm // tkv):
        for h in range(N_HEADS):
            vt_ref[u, h * VT_ROWS:h * VT_ROWS + V_HEAD, :] = (
                vt[h * V_HEAD:(h + 1) * V_HEAD, u * tkv:(u + 1) * tkv].astype(BF16))
            vt_ref[u, h * VT_ROWS + V_HEAD:(h + 1) * VT_ROWS, :] = ones_rows

    u_ref[...] = g_c * u_in
    gb_ref[...] = g_b
    mq = m_q * lax.rsqrt(_group_mean_sq(m_q, g64_ref) + EPS) * gmq_ref[...]
    mq_ref[...] = mq.astype(BF16)


def _premix(x, tables, w, tm):
    b, s, d = x.shape
    tkv = min(KEY_SUB, tm)
    cos, s1, s2 = tables
    row = lambda width: pl.BlockSpec((None, tm, width), lambda bi, i: (bi, i, 0))
    tab = pl.BlockSpec((tm, HEAD_SLOT), lambda bi, i: (i, 0))
    consts = [w["g_mix"], w["w_in"], w["g_q_lora"], w["w_uq"], w["gq_vec"], w["G2"], w["g_kv_lora"],
              w["gkr_vec"], w["w_uk"], w["gk_vec"], w["w_uv"], w["gmq_vec"], w["G64"]]
    out_shape = (
        jax.ShapeDtypeStruct((b, N_HEADS * HEAD_SLOT, s), BF16),
        jax.ShapeDtypeStruct((b, s, N_HEADS * HEAD_SLOT), BF16),
        jax.ShapeDtypeStruct((b, s // tkv, N_HEADS * VT_ROWS, tkv), BF16),
        jax.ShapeDtypeStruct((b, s, KV_LORA), F32),
        jax.ShapeDtypeStruct((b, QK_ROPE, s), F32),
        jax.ShapeDtypeStruct((b, s, CONV_WIDTH), F32),
        jax.ShapeDtypeStruct((b, s, CONV_WIDTH), F32),
        jax.ShapeDtypeStruct((b, s, MEM_WIDTH), BF16),
    )
    out_specs = (
        pl.BlockSpec((None, N_HEADS * HEAD_SLOT, tm), lambda bi, i: (bi, 0, i)),
        row(N_HEADS * HEAD_SLOT),
        pl.BlockSpec((None, tm // tkv, N_HEADS * VT_ROWS, tkv), lambda bi, i: (bi, i, 0, 0)),
        row(KV_LORA), pl.BlockSpec((None, QK_ROPE, tm), lambda bi, i: (bi, 0, i)),
        row(CONV_WIDTH), row(CONV_WIDTH), row(MEM_WIDTH),
    )
    return pl.pallas_call(
        _premix_body,
        grid=(b, s // tm),
        in_specs=[row(d), tab, tab, tab] + [_const_spec(c.shape) for c in consts],
        out_specs=out_specs,
        out_shape=out_shape,
        compiler_params=pltpu.CompilerParams(dimension_semantics=("parallel", "parallel")),
        name="premix",
    )(x, cos, s1, s2, *consts)


def _attn_body(qt_ref, k_ref, vt_ref, o_ref, s_even, s_odd):
    tq = qt_ref.shape[1]
    tk = vt_ref.shape[2]
    n_diag = tq // tk
    assert n_diag * tk == tq and n_diag % 2 == 0
    qi = pl.program_id(2)
    q_t = [qt_ref[j * HEAD_SLOT:(j + 1) * HEAD_SLOT, :] for j in range(2)]

    def scores(i, j, s_ref, lo=0):
        rows = pl.ds(pl.multiple_of(i * tk, tk), tk)
        s_ref[j, :, lo:] = _dot(k_ref[rows, j * HEAD_SLOT:(j + 1) * HEAD_SLOT], q_t[j][:, lo:])

    def absorb(i, j, s_ref, state, lo=0, diagonal=False):
        m, acc = state[2 * j], state[2 * j + 1]
        s_t = s_ref[j, :, lo:]
        if diagonal:
            kpos = lax.broadcasted_iota(jnp.int32, s_t.shape, 0)
            qpos = lax.broadcasted_iota(jnp.int32, s_t.shape, 1)
            s_t = jnp.where(kpos <= qpos, s_t, NEG)
        m_new = jnp.maximum(m[:, lo:], jnp.max(s_t, axis=0, keepdims=True))
        alpha = jnp.exp2(m[:, lo:] - m_new)
        p = jnp.exp2(s_t - m_new).astype(BF16)
        v = vt_ref[i, j * VT_ROWS:(j + 1) * VT_ROWS, :]
        acc_new = alpha * acc[:, lo:] + _dot(v, p)
        if lo:
            m_new = jnp.concatenate([m[:, :lo], m_new], axis=1)
            acc_new = jnp.concatenate([acc[:, :lo], acc_new], axis=1)
        state = list(state)
        state[2 * j], state[2 * j + 1] = m_new, acc_new
        return tuple(state)

    def pair(g, state):
        scores(2 * g + 1, 1, s_odd)
        state = absorb(2 * g, 0, s_even, state)
        scores(2 * g + 2, 0, s_even)
        state = absorb(2 * g, 1, s_even, state)
        scores(2 * g + 2, 1, s_even)
        state = absorb(2 * g + 1, 0, s_odd, state)
        scores(2 * g + 3, 0, s_odd)
        return absorb(2 * g + 1, 1, s_odd, state)

    scores(0, 0, s_even)
    scores(0, 1, s_even)
    scores(1, 0, s_odd)
    init = (jnp.full((1, tq), NEG, F32), jnp.zeros((VT_ROWS, tq), F32)) * 2
    n_pairs = qi * (n_diag // 2)
    state = lax.fori_loop(0, n_pairs // 2, lambda g, st: pair(2 * g + 1, pair(2 * g, st)), init)
    if n_diag % 4:
        state = lax.fori_loop(2 * (n_pairs // 2), n_pairs, pair, state)
    first = qi * n_diag
    scores(first + 1, 1, s_odd, tk)
    for d in range(n_diag):
        buf = s_odd if d % 2 else s_even
        for j in range(2):
            state = absorb(first + d, j, buf, state, d * tk, diagonal=True)
            if d + 2 < n_diag:
                scores(first + d + 2, j, buf, (d + 2) * tk)
    outs = [state[2 * j + 1][0:V_HEAD] / state[2 * j + 1][V_HEAD:V_HEAD + 1] for j in range(2)]
    o_ref[...] = jnp.concatenate(outs, axis=0).T


def _mla_prompt(qt, k, vt, tq):
    b, _, s = qt.shape
    nkt, tk = vt.shape[1], vt.shape[3]
    return pl.pallas_call(
        _attn_body,
        grid=(b, N_HEADS // 2, s // tq),
        in_specs=[pl.BlockSpec((None, 2 * HEAD_SLOT, tq), lambda bi, p, qi: (bi, p, qi)),
                  pl.BlockSpec((None, s, 2 * HEAD_SLOT), lambda bi, p, qi: (bi, 0, p)),
                  pl.BlockSpec((None, nkt, 2 * VT_ROWS, tk), lambda bi, p, qi: (bi, 0, p, 0))],
        out_specs=pl.BlockSpec((None, tq, 2 * V_HEAD), lambda bi, p, qi: (bi, qi, p)),
        out_shape=jax.ShapeDtypeStruct((b, s, MLA_WIDTH), F32),
        scratch_shapes=[pltpu.VMEM((2, tk, tq), F32), pltpu.VMEM((2, tk, tq), F32)],
        compiler_params=pltpu.CompilerParams(
            dimension_semantics=("parallel", "parallel", "arbitrary")),
        name="mla_prompt",
    )(qt, k, vt)


def _memkv_body(mem_ref, g_ref, wk_ref, wv_ref, g64_ref, gmk_ref, k_ref, v_ref):
    hm = _rms(mem_ref[...], g_ref[...]).astype(BF16)
    k = _dot(hm, wk_ref[...])
    k_ref[...] = k * lax.rsqrt(_group_mean_sq(k, g64_ref) + EPS) * gmk_ref[...]
    v_ref[...] = _dot(hm, wv_ref[...])


def _mem_kv(mem, w):
    b, n, d = mem.shape
    consts = [w["g_mem"], w["w_mem_k"], w["w_mem_v"], w["G64"], w["gmk_vec"]]
    blk = pl.BlockSpec((None, n, MEM_WIDTH), lambda bi: (bi, 0, 0))
    return pl.pallas_call(
        _memkv_body,
        grid=(b,),
        in_specs=[pl.BlockSpec((None, n, d), lambda bi: (bi, 0, 0))]
        + [_const_spec(c.shape) for c in consts],
        out_specs=(blk, blk),
        out_shape=(jax.ShapeDtypeStruct((b, n, MEM_WIDTH), F32),) * 2,
        compiler_params=pltpu.CompilerParams(dimension_semantics=("parallel",)),
        name="mem_kv",
    )(mem, *consts)


def _out_proj(x, o_mla, y_conv, o_mem, g1_ref, g2_ref, g3_ref, wo_ref):
    o = _dot(_rms(o_mla, g1_ref[...]).astype(BF16), wo_ref[0:MLA_WIDTH, :])
    o += _dot(_rms(y_conv, g2_ref[...]).astype(BF16), wo_ref[MLA_WIDTH:MLA_WIDTH + CONV_WIDTH, :])
    o += _dot(_rms(o_mem, g3_ref[...]).astype(BF16), wo_ref[MLA_WIDTH + CONV_WIDTH:, :])
    return x + o


def _postmix_body(x_ref, omla_ref, u_ref, uprev_ref, gb_ref, mq_ref, mk_ref, mv_ref, cw_ref,
                  g1_ref, g2_ref, g3_ref, wo_ref, o_ref, ubuf):
    tm = x_ref.shape[0]
    n_mem = mv_ref.shape[0] // MEM_HEADS
    u = u_ref[...]
    ubuf[0:8, :] = jnp.where(pl.program_id(1) == 0, 0.0, uprev_ref[...])
    ubuf[8:8 + tm, :] = u
    cw = cw_ref[...]
    y = cw[0:1] * ubuf[pl.ds(6, tm), :] + cw[1:2] * ubuf[pl.ds(7, tm), :] + cw[2:3] * u
    y = gb_ref[...] * y
    s = _dot(mq_ref[...], mk_ref[...])
    ps = []
    for h in range(MEM_HEADS):
        sh = s[:, h * n_mem:(h + 1) * n_mem]
        p = jnp.exp2(sh - jnp.max(sh, axis=-1, keepdims=True))
        ps.append((p / jnp.sum(p, axis=-1, keepdims=True)).astype(BF16))
    o_mem = _dot(jnp.concatenate(ps, axis=1), mv_ref[...])
    o_ref[...] = _out_proj(x_ref[...], omla_ref[...], y, o_mem, g1_ref, g2_ref, g3_ref, wo_ref)


def _postmix(x, o_mla, u, gb, mq, mk_bd, mv_bd, w, tm):
    b, s, d = x.shape
    row = lambda width: pl.BlockSpec((None, tm, width), lambda bi, i: (bi, i, 0))
    prev = pl.BlockSpec((None, 8, CONV_WIDTH), lambda bi, i: (bi, jnp.maximum(i * (tm // 8) - 1, 0), 0))
    per_b = lambda a: pl.BlockSpec((None,) + a.shape[1:], lambda bi, i: (bi, 0, 0))
    consts = [w["conv_w"], w["g_out_mla"], w["g_out_conv"], w["g_out_mem"], w["w_o"]]
    return pl.pallas_call(
        _postmix_body,
        grid=(b, s // tm),
        in_specs=[row(d), row(MLA_WIDTH), row(CONV_WIDTH), prev, row(CONV_WIDTH), row(MEM_WIDTH),
                  per_b(mk_bd), per_b(mv_bd)] + [_const_spec(c.shape) for c in consts],
        out_specs=row(d),
        out_shape=jax.ShapeDtypeStruct((b, s, d), F32),
        scratch_shapes=[pltpu.VMEM((tm + 8, CONV_WIDTH), F32)],
        compiler_params=pltpu.CompilerParams(dimension_semantics=("parallel", "parallel")),
        name="postmix",
    )(x, o_mla, u, u, gb, mq, mk_bd, mv_bd, *consts)


def _absorb_body(q_ref, gk_ref, wt_ref, a_ref):
    for h in range(N_HEADS):
        qh = q_ref[:, h * HEAD_SLOT:(h + 1) * HEAD_SLOT].astype(F32) * gk_ref[...]
        a_ref[h] = _dot(qh.astype(BF16), wt_ref[h])


def _absorb(q, gk_slot, wuk_t):
    n = q.shape[0]
    return pl.pallas_call(
        _absorb_body,
        out_shape=jax.ShapeDtypeStruct((N_HEADS, n, KV_LORA), F32),
        name="absorb_q",
    )(q, gk_slot, wuk_t)


def _paged_body(pt_ref, wt_ref, a_ref, qr_ref, ckv_hbm, kr_hbm, acc_ref, m_ref, l_ref,
                xs, krs, lhs, s_scr, sem, *, n_chunks, ppc, page_base):
    nk = ppc * PAGE_SIZE
    nope = N_HEADS * QK_NOPE
    b = pl.program_id(0)
    last = pl.num_programs(0) * n_chunks - 1

    def copies(t, slot):
        out = []
        for i in range(ppc):
            pg = pt_ref[t * ppc + i] + page_base
            keys = pl.ds(i * PAGE_SIZE, PAGE_SIZE)
            out.append(pltpu.make_async_copy(ckv_hbm.at[pg], xs.at[slot, keys, :], sem.at[slot, 0]))
            out.append(pltpu.make_async_copy(kr_hbm.at[pg], krs.at[slot, :, keys], sem.at[slot, 1]))
        return out

    def start(t, slot):
        for cp in copies(t, slot):
            cp.start()

    def wait(t, slot):
        for cp in copies(t, slot):
            cp.wait()

    def scores(t, slot):
        e = t // n_chunks
        wait(t, slot)
        lhs[nope:, :] = jnp.concatenate([a_ref[e], jnp.zeros((8, KV_LORA), F32)], axis=0).astype(BF16)
        qr = jnp.concatenate([qr_ref[e], jnp.zeros((8, QK_ROPE), F32)], axis=0).astype(BF16)
        out = _dot_nt(lhs[...], xs[slot].astype(BF16))
        k_t = out[0:nope]
        n = jnp.sum((k_t * k_t).reshape(N_HEADS, QK_NOPE, nk), axis=1)
        rope = _dot(qr, krs[slot].astype(BF16))[0:N_HEADS]
        return out[nope:nope + N_HEADS] * lax.rsqrt(n * (1.0 / QK_NOPE) + EPS) + rope

    @pl.when(b == 0)
    def _():
        lhs[0:nope, :] = wt_ref[...]
        start(0, 0)
        start(1, 1)
        s_scr[...] = scores(0, 0)

    def step(c, carry):
        m, l, acc, s = carry
        t = b * n_chunks + c
        start(jnp.minimum(t + 2, last), (t + 2) % PAGE_SLOTS)
        s_next = scores(jnp.minimum(t + 1, last), (t + 1) % PAGE_SLOTS)
        m_new = jnp.maximum(m, jnp.max(s, axis=-1, keepdims=True))
        alpha = jnp.exp2(m - m_new)
        p = jnp.exp2(s - m_new)
        l = alpha * l + jnp.sum(p, axis=-1, keepdims=True)
        p16 = jnp.concatenate([p, jnp.zeros_like(p)], axis=0).astype(BF16)
        acc = alpha * acc + _dot(p16, xs[t % PAGE_SLOTS].astype(BF16))[0:N_HEADS]
        return m_new, l, acc, s_next

    init = (jnp.full((N_HEADS, 1), NEG, F32), jnp.zeros((N_HEADS, 1), F32),
            jnp.zeros((N_HEADS, KV_LORA), F32), s_scr[...])
    m, l, acc, s = lax.fori_loop(0, n_chunks, step, init)
    s_scr[...] = s
    acc_ref[0] = acc
    m_ref[0] = jnp.broadcast_to(m, (N_HEADS, HEAD_SLOT))
    l_ref[0] = jnp.broadcast_to(l, (N_HEADS, HEAD_SLOT))

    @pl.when(b == pl.num_programs(0) - 1)
    def _():
        wait(last, (last + 2) % PAGE_SLOTS)


def _paged_mla(page_table, wuk_t2d, a, qr, cache_ckv, cache_krope, page_base):
    nb, n_pages = page_table.shape
    ppc = min(PAGES_PER_CHUNK, n_pages)
    n_chunks = n_pages // ppc
    assert n_chunks * ppc == n_pages and nb * n_chunks >= 2
    nk = ppc * PAGE_SIZE
    vmem = pl.BlockSpec(memory_space=pltpu.VMEM)
    per_b = pl.BlockSpec((1, N_HEADS, HEAD_SLOT), lambda bi: (bi, 0, 0))
    out = jax.ShapeDtypeStruct((nb, N_HEADS, HEAD_SLOT), F32)
    return pl.pallas_call(
        functools.partial(_paged_body, n_chunks=n_chunks, ppc=ppc, page_base=page_base),
        grid=(nb,),
        in_specs=[pl.BlockSpec(memory_space=pltpu.SMEM), vmem, vmem, vmem,
                  pl.BlockSpec(memory_space=pl.ANY), pl.BlockSpec(memory_space=pl.ANY)],
        out_specs=(per_b, per_b, per_b),
        out_shape=(out, out, out),
        scratch_shapes=[pltpu.VMEM((PAGE_SLOTS, nk, KV_LORA), F32),
                        pltpu.VMEM((PAGE_SLOTS, QK_ROPE, nk), F32),
                        pltpu.VMEM((N_HEADS * QK_NOPE + 16, KV_LORA), BF16),
                        pltpu.VMEM((N_HEADS, nk), F32),
                        pltpu.SemaphoreType.DMA((PAGE_SLOTS, 2))],
        compiler_params=pltpu.CompilerParams(dimension_semantics=("arbitrary",)),
        name="paged_mla",
    )(page_table.reshape(-1), wuk_t2d, a, qr, cache_ckv, cache_krope)


def _smem_body(mq_ref, kt_ref, vt_ref, o_ref):
    bc, w, n_mem = kt_ref.shape
    step = pl.program_id(0)
    lane = lax.broadcasted_iota(jnp.int32, o_ref.shape, 1)

    @pl.when(step == 0)
    def _():
        o_ref[...] = jnp.zeros_like(o_ref)

    out = o_ref[...]
    for i in range(bc):
        q = jnp.concatenate([mq_ref[i]] * (n_mem // HEAD_SLOT), axis=1)
        s = jnp.sum((kt_ref[i] * q).reshape(MEM_HEADS, MEM_HEAD_DIM, n_mem), axis=1)
        p = jnp.exp2(s - jnp.max(s, axis=-1, keepdims=True))
        p = p / jnp.sum(p, axis=-1, keepdims=True)
        col = jnp.concatenate(
            [jnp.sum(vt_ref[i, h * MEM_HEAD_DIM:(h + 1) * MEM_HEAD_DIM, :] * p[h:h + 1, :],
                     axis=-1, keepdims=True) for h in range(MEM_HEADS)], axis=0)
        out = jnp.where(lane == step * bc + i, col, out)
    o_ref[...] = out


def _sample_mem_attend(mq_lanes, kt, vt, bc=8):
    nb, w, n_mem = kt.shape
    return pl.pallas_call(
        _smem_body,
        grid=(nb // bc,),
        in_specs=[pl.BlockSpec((bc, w, HEAD_SLOT), lambda i: (i, 0, 0)),
                  pl.BlockSpec((bc, w, n_mem), lambda i: (i, 0, 0)),
                  pl.BlockSpec((bc, w, n_mem), lambda i: (i, 0, 0))],
        out_specs=pl.BlockSpec((w, nb), lambda i: (0, 0)),
        out_shape=jax.ShapeDtypeStruct((w, nb), F32),
        compiler_params=pltpu.CompilerParams(dimension_semantics=("arbitrary",)),
        name="sample_mem_attend",
    )(mq_lanes, kt, vt)


def _spost_body(x_ref, q_ref, k_ref, ckv_ref, acc_ref, m_ref, l_ref, wuv_ref, u_ref, gb_ref,
                s0_ref, s1_ref, cw_ref, omem_ref, g1_ref, g2_ref, g3_ref, wo_ref, o_ref):
    ckv = ckv_ref[...]
    o_mla = jnp.zeros((x_ref.shape[0], MLA_WIDTH), F32)
    for h in range(N_HEADS):
        sl = slice(h * HEAD_SLOT, (h + 1) * HEAD_SLOT)
        s_new = jnp.sum(q_ref[:, sl].astype(F32) * k_ref[:, sl].astype(F32), axis=-1, keepdims=True)
        m_old = m_ref[h][:, 0:1]
        l_old = l_ref[h][:, 0:1]
        m_new = jnp.maximum(m_old, s_new)
        alpha = jnp.exp2(m_old - m_new)
        p_new = jnp.exp2(s_new - m_new)
        o_lat = (acc_ref[h] * alpha + p_new * ckv) / (l_old * alpha + p_new)
        o_mla += _dot(o_lat.astype(BF16), wuv_ref[h])
    cw = cw_ref[...]
    y = gb_ref[...] * (cw[0:1] * s0_ref[...] + cw[1:2] * s1_ref[...] + cw[2:3] * u_ref[...])
    o_ref[...] = _out_proj(x_ref[...], o_mla, y, omem_ref[...], g1_ref, g2_ref, g3_ref, wo_ref)


def _sample_postmix(*args):
    n = args[0].shape[0]
    return pl.pallas_call(
        _spost_body,
        out_shape=jax.ShapeDtypeStruct((n, D_MODEL), F32),
        name="sample_postmix",
    )(*args)


def _rope_tables(pos):
    inv_freq = ROPE_THETA ** (-jnp.arange(0, QK_ROPE, 2, dtype=F32) / QK_ROPE)
    ang = pos.astype(F32)[:, None] * inv_freq[None, :]
    cos, sin = jnp.cos(ang), jnp.sin(ang)
    n = pos.shape[0]
    zeros = lambda w_: jnp.zeros((n, w_), F32)
    tail = HEAD_SLOT - ROPE_LO - QK_ROPE
    c = jnp.concatenate([jnp.ones((n, ROPE_LO), F32), cos, cos, zeros(tail)], axis=1)
    s1 = jnp.concatenate([zeros(ROPE_LO), -sin, zeros(HALF_ROPE), zeros(tail)], axis=1)
    s2 = jnp.concatenate([zeros(ROPE_LO), zeros(HALF_ROPE), sin, zeros(tail)], axis=1)
    return c, s1, s2


def _block_diag_avg(sizes, width):
    idx = jnp.arange(width)
    gid = jnp.full((width,), -1, jnp.int32)
    scale = jnp.zeros((width,), F32)
    lo = 0
    for g, sz in enumerate(sizes):
        inside = (idx >= lo) & (idx < lo + sz)
        gid = jnp.where(inside, g, gid)
        scale = jnp.where(inside, 1.0 / sz, scale)
        lo += sz
    same = (gid[:, None] == gid[None, :]) & (gid[:, None] >= 0)
    return jnp.where(same, scale[None, :], 0.0).astype(BF16)


def _layer_weights(l, p):
    pad_last = lambda a, n: jnp.pad(a, [(0, 0)] * (a.ndim - 1) + [(0, n - a.shape[-1])])
    row = lambda v: v.reshape(1, -1).astype(F32)
    w_in = p["w_in"][l]
    zc = lambda n: jnp.zeros((D_MODEL, n), F32)
    w_in_p = jnp.concatenate([w_in[:, 0:384], zc(ROPE_LO), w_in[:, 384:416],
                              zc(HEAD_SLOT - ROPE_LO - QK_ROPE), w_in[:, 416:]], axis=1)
    slot_gain = lambda nope, rope: jnp.concatenate(
        [nope, rope, jnp.zeros((HEAD_SLOT - QK_NOPE - QK_ROPE,), F32)])
    zeros_n = jnp.zeros((QK_NOPE,), F32)
    zeros_r = jnp.zeros((QK_ROPE,), F32)
    g_kn = p["g_kn"][l]
    w_uk = p["w_uk"][l]
    g_slot = _block_diag_avg((QK_NOPE, QK_ROPE), HEAD_SLOT).astype(F32)
    g2 = jnp.kron(jnp.eye(2, dtype=F32), g_slot).astype(BF16)
    wuv = p["w_uv"][l]
    wuv_bd = jnp.stack([jnp.pad(wuv[:, h, :], ((0, 0), (h * V_HEAD, MLA_WIDTH - (h + 1) * V_HEAD)))
                        for h in range(N_HEADS)])
    return {
        "g_ffn1": row(p["g_ffn1"][l]), "w1_gate": p["w1_gate"][l].astype(BF16),
        "w1_up": p["w1_up"][l].astype(BF16), "w1_down": p["w1_down"][l].astype(BF16),
        "g_ffn2": row(p["g_ffn2"][l]), "w2_gate": p["w2_gate"][l].astype(BF16),
        "w2_up": p["w2_up"][l].astype(BF16), "w2_down": p["w2_down"][l].astype(BF16),
        "g_mix": row(p["g_mix"][l]), "w_in": w_in_p.astype(BF16),
        "g_q_lora": row(p["g_q_lora"][l]),
        "w_uq": pad_last(p["w_uq"][l].reshape(Q_LORA, N_HEADS, QK_NOPE + QK_ROPE), HEAD_SLOT)
        .reshape(Q_LORA, N_HEADS * HEAD_SLOT).astype(BF16),
        "gq_vec": row(jnp.tile(slot_gain(p["g_qn"][l], p["g_qr"][l]), N_HEADS) * (MLA_SCALE * LOG2E)),
        "G2": g2,
        "g_kv_lora": row(p["g_kv_lora"][l]),
        "gkr_vec": row(slot_gain(zeros_n, p["g_kr"][l])),
        "w_uk": pad_last(w_uk, HEAD_SLOT).reshape(KV_LORA, N_HEADS * HEAD_SLOT).astype(BF16),
        "gk_vec": row(jnp.tile(slot_gain(g_kn, zeros_r), N_HEADS)),
        "gk_slot": row(slot_gain(g_kn, zeros_r)),
        "w_uv": wuv.reshape(KV_LORA, MLA_WIDTH).astype(BF16),
        "wuv_bd": wuv_bd.astype(BF16),
        "wuk_t_pad": jnp.pad(jnp.transpose(w_uk, (1, 2, 0)), ((0, 0), (0, HEAD_SLOT - QK_NOPE), (0, 0)))
        .astype(BF16),
        "wuk_t2d": jnp.transpose(w_uk, (1, 2, 0)).reshape(N_HEADS * QK_NOPE, KV_LORA).astype(BF16),
        "gmq_vec": row(jnp.tile(p["g_mq"][l], MEM_HEADS) * (MEM_SCALE * LOG2E)),
        "G64": _block_diag_avg((MEM_HEAD_DIM,) * MEM_HEADS, MEM_WIDTH),
        "g_mem": row(p["g_mem"][l]), "w_mem_k": p["w_mem_k"][l].astype(BF16),
        "w_mem_v": p["w_mem_v"][l].astype(BF16),
        "gmk_vec": row(jnp.tile(p["g_mk"][l], MEM_HEADS)),
        "conv_w": p["conv_w"][l].astype(F32),
        "g_out_mla": row(p["g_out_mla"][l]), "g_out_conv": row(p["g_out_conv"][l]),
        "g_out_mem": row(p["g_out_mem"][l]), "w_o": p["w_o"][l].astype(BF16),
    }


def _mem_block_diag(mk, mv):
    b, n, _ = mk.shape
    k4 = mk.reshape(b, n, MEM_HEADS, MEM_HEAD_DIM)
    v4 = mv.reshape(b, n, MEM_HEADS, MEM_HEAD_DIM)
    eye = jnp.eye(MEM_HEADS, dtype=F32)
    k_bd = jnp.einsum("bnhd,hg->bhdgn", k4, eye).reshape(b, MEM_WIDTH, MEM_HEADS * n)
    v_bd = jnp.einsum("bnhd,hg->bhngd", v4, eye).reshape(b, MEM_HEADS * n, MEM_WIDTH)
    return k_bd.astype(BF16), v_bd.astype(BF16)


def _prompt_layer(x, mem, w, tables, tm):
    b, s, d = x.shape
    x = _ffn_half(x.reshape(b * s, d), w["g_ffn1"], w["w1_gate"], w["w1_up"], w["w1_down"],
                  tm).reshape(b, s, d)
    qt, k, vt, ckv, kr, u, gb, mq = _premix(x, tables, w, tm)
    o_mla = _mla_prompt(qt, k, vt, min(QUERY_TILE, s))
    mk, mv = _mem_kv(mem, w)
    mk_bd, mv_bd = _mem_block_diag(mk, mv)
    x = _postmix(x, o_mla, u, gb, mq, mk_bd, mv_bd, w, tm)
    x = _ffn_half(x.reshape(b * s, d), w["g_ffn2"], w["w2_gate"], w["w2_up"], w["w2_down"],
                  tm).reshape(b, s, d)
    return x, ckv, jnp.swapaxes(kr, 1, 2), u, mk, mv


def _sample_layer(x, w, tables, cache_ckv, cache_krope, page_base, page_table, state, mem_k, mem_v):
    n = x.shape[0]
    x = _ffn_half(x, w["g_ffn1"], w["w1_gate"], w["w1_up"], w["w1_down"], n)
    qt, k, _, ckv, kr, u, gb, mq = _premix(x[None], tables, w, n)
    q = qt[0].T
    a = _absorb(q, w["gk_slot"], w["wuk_t_pad"])
    qr = q.reshape(n, N_HEADS, HEAD_SLOT)[:, :, ROPE_LO:ROPE_LO + QK_ROPE].astype(F32)
    acc, m, l = _paged_mla(page_table, w["wuk_t2d"], jnp.transpose(a, (1, 0, 2)), qr,
                           cache_ckv, cache_krope, page_base)
    head_major = lambda t: jnp.transpose(t, (1, 0, 2))
    feature_major = lambda t: jnp.transpose(t, (0, 2, 3, 1)).reshape(n, MEM_WIDTH, -1)
    mq_lanes = jnp.broadcast_to(mq[0].astype(F32)[:, :, None], (n, MEM_WIDTH, HEAD_SLOT))
    o_mem = _sample_mem_attend(mq_lanes, feature_major(mem_k), feature_major(mem_v)).T
    x = _sample_postmix(x, q, k[0], ckv[0], head_major(acc), head_major(m), head_major(l),
                        w["wuv_bd"], u[0], gb[0], state[:, 0, :], state[:, 1, :], w["conv_w"], o_mem,
                        w["g_out_mla"], w["g_out_conv"], w["g_out_mem"], w["w_o"])
    x = _ffn_half(x, w["g_ffn2"], w["w2_gate"], w["w2_up"], w["w2_down"], n)
    return x, ckv[0], kr[0].T, jnp.stack([state[:, 1, :], u[0]], axis=1)


def kernel(x_prompt, mem_prompt, x_sample, cache_ckv, cache_krope, page_table, state_conv, cache_mem_k,
           cache_mem_v, g_ffn1, w1_gate, w1_up, w1_down, g_mix, w_in, g_q_lora, w_uq, g_qn, g_qr,
           g_kv_lora, w_uk, w_uv, g_kn, g_kr, conv_w, g_mem, w_mem_k, w_mem_v, g_mq, g_mk, g_out_mla,
           g_out_conv, g_out_mem, w_o, g_ffn2, w2_gate, w2_up, w2_down):
    params = dict(g_ffn1=g_ffn1, w1_gate=w1_gate, w1_up=w1_up, w1_down=w1_down, g_mix=g_mix, w_in=w_in,
                  g_q_lora=g_q_lora, w_uq=w_uq, g_qn=g_qn, g_qr=g_qr, g_kv_lora=g_kv_lora, w_uk=w_uk,
                  w_uv=w_uv, g_kn=g_kn, g_kr=g_kr, conv_w=conv_w, g_mem=g_mem, w_mem_k=w_mem_k,
                  w_mem_v=w_mem_v, g_mq=g_mq, g_mk=g_mk, g_out_mla=g_out_mla, g_out_conv=g_out_conv,
                  g_out_mem=g_out_mem, w_o=w_o, g_ffn2=g_ffn2, w2_gate=w2_gate, w2_up=w2_up,
                  w2_down=w2_down)
    depth = w_in.shape[0]
    b, s, _ = x_prompt.shape
    nb, dec_seq, _ = x_sample.shape
    assert dec_seq == 1
    n_phys = cache_ckv.shape[1]
    tm = min(ROW_TILE, s)
    tab_p = _rope_tables(jnp.arange(s))
    tab_s = _rope_tables(jnp.full((nb,), PAST_LEN, jnp.int32))
    ckv_pages = cache_ckv.reshape(depth * n_phys, PAGE_SIZE, KV_LORA)
    kr_pages = jnp.swapaxes(cache_krope, 2, 3).reshape(depth * n_phys, QK_ROPE, PAGE_SIZE)

    xp, xs = x_prompt, x_sample.reshape(nb, D_MODEL)
    outs_p, outs_s = [], []
    for l in range(depth):
        w = _layer_weights(l, params)
        xp, ckv, kr, u, mk, mv = _prompt_layer(xp, mem_prompt, w, tab_p, tm)
        outs_p.append((ckv, kr, u[:, -(CONV_K - 1):], mk.reshape(b, -1, MEM_HEADS, MEM_HEAD_DIM),
                       mv.reshape(b, -1, MEM_HEADS, MEM_HEAD_DIM)))
        xs, ckv_s, kr_s, conv_s = _sample_layer(xs, w, tab_s, ckv_pages, kr_pages, l * n_phys, page_table,
                                                state_conv[l], cache_mem_k[l], cache_mem_v[l])
        outs_s.append((ckv_s[:, None, :], kr_s[:, None, :], conv_s))
    stack = lambda items, i: jnp.stack([it[i] for it in items])
    return (xp, xs.reshape(nb, 1, D_MODEL), stack(outs_p, 0), stack(outs_p, 1), stack(outs_p, 2),
            stack(outs_p, 3), stack(outs_p, 4), stack(outs_s, 0), stack(outs_s, 1), stack(outs_s, 2))
```

```python
import functools

import jax
import jax.numpy as jnp
from jax import lax
from jax.experimental import pallas as pl
from jax.experimental.pallas import tpu as pltpu

F32 = jnp.float32
BF16 = jnp.bfloat16

D_MODEL = 1024
N_HEADS = 8
Q_LORA = 256
KV_LORA = 128
QK_NOPE = 64
QK_ROPE = 32
V_HEAD = 64
MLA_WIDTH = N_HEADS * V_HEAD
CONV_WIDTH = 256
CONV_K = 3
MEM_HEADS = 4
MEM_HEAD_DIM = 64
MEM_WIDTH = MEM_HEADS * MEM_HEAD_DIM
D_FF = 2816
ROPE_THETA = 10000.0
EPS = 1e-6
PAST_LEN = 16384
PAGE_SIZE = 128
MLA_SCALE = (QK_NOPE + QK_ROPE) ** -0.5
MEM_SCALE = MEM_HEAD_DIM ** -0.5
LOG2E = 1.4426950408889634

HEAD_SLOT = 128
ROPE_LO = QK_NOPE
HALF_ROPE = QK_ROPE // 2
Z_WIDTH = 1536
VT_ROWS = 80
NEG = -1e30
PAGES_PER_CHUNK = 64
PAGE_SLOTS = 4
ROW_TILE = 512
KEY_SUB = 256
QUERY_TILE = 1024
HEADS_PER_STEP = 2


def _rms(x, g):
    ms = jnp.mean(x * x, axis=-1, keepdims=True)
    return x * lax.rsqrt(ms + EPS) * g


def _dot(a, b):
    return jnp.dot(a, b, preferred_element_type=F32)


def _dot_nt(a, b):
    return lax.dot_general(a, b, (((1,), (1,)), ((), ())), preferred_element_type=F32)


def _group_mean_sq(x, g_ref):
    x2 = (x * x).astype(BF16)
    g = g_ref[...]
    cols = [_dot(x2[:, j * 256:(j + 1) * 256], g) for j in range(x.shape[1] // 256)]
    return cols[0] if len(cols) == 1 else jnp.concatenate(cols, axis=1)


def _rope_slab(v, cos, s1, s2):
    return (v * cos + pltpu.roll(v, HEAD_SLOT - HALF_ROPE, 1) * s1
            + pltpu.roll(v, HALF_ROPE, 1) * s2)


def _const_spec(shape):
    nd = len(shape)
    return pl.BlockSpec(shape, lambda *_: (0,) * nd, pipeline_mode=pl.Buffered(1))


def _swiglu_half_step(x, g_ref, wg_ref, wu_ref, wd_ref):
    xn = _rms(x, g_ref[...]).astype(BF16)
    h = _dot(xn, wg_ref[...])
    u = _dot(xn, wu_ref[...])
    a = (h / (1.0 + jnp.exp(-h)) * u).astype(BF16)
    return x + 0.5 * _dot(a, wd_ref[...])


def _ffn_body(x_ref, g_ref, wg_ref, wu_ref, wd_ref, o_ref):
    o_ref[...] = _swiglu_half_step(x_ref[...], g_ref, wg_ref, wu_ref, wd_ref)


def _ffn_half(x, g, wg, wu, wd, tm):
    n, d = x.shape
    dff = wg.shape[1]
    return pl.pallas_call(
        _ffn_body,
        grid=(n // tm,),
        in_specs=[pl.BlockSpec((tm, d), lambda i: (i, 0)),
                  _const_spec((1, d)), _const_spec((d, dff)), _const_spec((d, dff)),
                  _const_spec((dff, d))],
        out_specs=pl.BlockSpec((tm, d), lambda i: (i, 0)),
        out_shape=jax.ShapeDtypeStruct((n, d), F32),
        compiler_params=pltpu.CompilerParams(dimension_semantics=("parallel",)),
        name="ffn_half",
    )(x, g, wg, wu, wd)


def _premix_body(x_ref, cos_ref, s1_ref, s2_ref, gmix_ref, win_ref, gql_ref, wuq_ref, gq_ref,
                 g2_ref, gkvl_ref, gkr_ref, wuk_ref, gk_ref, wuv_ref, gmq_ref, g64_ref,
                 qt_ref, k_ref, vt_ref, ckv_ref, kr_ref, u_ref, gb_ref, mq_ref):
    tm = x_ref.shape[0]
    hn = _rms(x_ref[...], gmix_ref[...]).astype(BF16)
    z = _dot(hn, win_ref[...])
    c_q, c_kv, k_r = z[:, 0:256], z[:, 256:384], z[:, 384:512]
    u_in, g_b, g_c, m_q = z[:, 512:768], z[:, 768:1024], z[:, 1024:1280], z[:, 1280:1536]
    cos, s1, s2 = cos_ref[...], s1_ref[...], s2_ref[...]

    q = _dot(_rms(c_q, gql_ref[...]).astype(BF16), wuq_ref[...])
    q = q * lax.rsqrt(_group_mean_sq(q, g2_ref) + EPS) * gq_ref[...]
    q = jnp.concatenate(
        [_rope_slab(q[:, h * HEAD_SLOT:(h + 1) * HEAD_SLOT], cos, s1, s2) for h in range(N_HEADS)],
        axis=1)
    qt_ref[...] = q.T.astype(BF16)

    ckv = _rms(c_kv, gkvl_ref[...])
    ckv_ref[...] = ckv
    ckv16 = ckv.astype(BF16)
    kr_ms = jnp.sum(k_r * k_r, axis=-1, keepdims=True) * (1.0 / QK_ROPE)
    krr = _rope_slab(k_r * lax.rsqrt(kr_ms + EPS) * gkr_ref[...], cos, s1, s2)
    kr_ref[...] = krr.T[ROPE_LO:ROPE_LO + QK_ROPE, :]
    kk = _dot(ckv16, wuk_ref[...])
    kk = kk * lax.rsqrt(_group_mean_sq(kk, g2_ref) + EPS) * gk_ref[...]
    k_ref[...] = jnp.concatenate(
        [kk[:, h * HEAD_SLOT:(h + 1) * HEAD_SLOT] + krr for h in range(N_HEADS)],
        axis=1).astype(BF16)
    vt = _dot(ckv16, wuv_ref[...]).T
    tkv = vt_ref.shape[2]
    ones_rows = jnp.where(lax.broadcasted_iota(jnp.int32, (VT_ROWS - V_HEAD, tkv), 0) == 0,
                          1.0, 0.0).astype(BF16)
    for u in range(tm // tkv):
        for h in range(N_HEADS):
            vt_ref[u, h * VT_ROWS:h * VT_ROWS + V_HEAD, :] = (
                vt[h * V_HEAD:(h + 1) * V_HEAD, u * tkv:(u + 1) * tkv].astype(BF16))
            vt_ref[u, h * VT_ROWS + V_HEAD:(h + 1) * VT_ROWS, :] = ones_rows

    u_ref[...] = g_c * u_in
    gb_ref[...] = g_b
    mq = m_q * lax.rsqrt(_group_mean_sq(m_q, g64_ref) + EPS) * gmq_ref[...]
    mq_ref[...] = mq.astype(BF16)


def _premix(x, tables, w, tm):
    b, s, d = x.shape
    tkv = min(KEY_SUB, tm)
    cos, s1, s2 = tables
    row = lambda width: pl.BlockSpec((None, tm, width), lambda bi, i: (bi, i, 0))
    tab = pl.BlockSpec((tm, HEAD_SLOT), lambda bi, i: (i, 0))
    consts = [w["g_mix"], w["w_in"], w["g_q_lora"], w["w_uq"], w["gq_vec"], w["G2"], w["g_kv_lora"],
              w["gkr_vec"], w["w_uk"], w["gk_vec"], w["w_uv"], w["gmq_vec"], w["G64"]]
    out_shape = (
        jax.ShapeDtypeStruct((b, N_HEADS * HEAD_SLOT, s), BF16),
        jax.ShapeDtypeStruct((b, s, N_HEADS * HEAD_SLOT), BF16),
        jax.ShapeDtypeStruct((b, s // tkv, N_HEADS * VT_ROWS, tkv), BF16),
        jax.ShapeDtypeStruct((b, s, KV_LORA), F32),
        jax.ShapeDtypeStruct((b, QK_ROPE, s), F32),
        jax.ShapeDtypeStruct((b, s, CONV_WIDTH), F32),
        jax.ShapeDtypeStruct((b, s, CONV_WIDTH), F32),
        jax.ShapeDtypeStruct((b, s, MEM_WIDTH), BF16),
    )
    out_specs = (
        pl.BlockSpec((None, N_HEADS * HEAD_SLOT, tm), lambda bi, i: (bi, 0, i)),
        row(N_HEADS * HEAD_SLOT),
        pl.BlockSpec((None, tm // tkv, N_HEADS * VT_ROWS, tkv), lambda bi, i: (bi, i, 0, 0)),
        row(KV_LORA), pl.BlockSpec((None, QK_ROPE, tm), lambda bi, i: (bi, 0, i)),
        row(CONV_WIDTH), row(CONV_WIDTH), row(MEM_WIDTH),
    )
    return pl.pallas_call(
        _premix_body,
        grid=(b, s // tm),
        in_specs=[row(d), tab, tab, tab] + [_const_spec(c.shape) for c in consts],
        out_specs=out_specs,
        out_shape=out_shape,
        compiler_params=pltpu.CompilerParams(dimension_semantics=("parallel", "parallel")),
        name="premix",
    )(x, cos, s1, s2, *consts)


def _attn_body(qt_ref, k_ref, vt_ref, o_ref, s_even, s_odd, m_ref, acc_ref):
    tq = qt_ref.shape[1]
    tk = vt_ref.shape[2]
    nh = qt_ref.shape[0] // HEAD_SLOT
    n_diag = tq // tk
    assert n_diag * tk == tq and n_diag % 2 == 0
    qi = pl.program_id(2)
    q_t = [qt_ref[j * HEAD_SLOT:(j + 1) * HEAD_SLOT, :] for j in range(nh)]

    def scores(i, j, s_ref, lo=0):
        rows = pl.ds(pl.multiple_of(i * tk, tk), tk)
        s_ref[j, :, lo:] = _dot(k_ref[rows, j * HEAD_SLOT:(j + 1) * HEAD_SLOT], q_t[j][:, lo:])

    def absorb(i, j, s_ref, lo=0, diagonal=False):
        v = vt_ref[i, j * VT_ROWS:(j + 1) * VT_ROWS, :]
        for c in range(lo // tk, n_diag):
            cols = slice(c * tk, (c + 1) * tk)
            s_t = s_ref[j, :, cols]
            if diagonal and c * tk == lo:
                kpos = lax.broadcasted_iota(jnp.int32, s_t.shape, 0)
                qpos = lax.broadcasted_iota(jnp.int32, s_t.shape, 1)
                s_t = jnp.where(kpos <= qpos, s_t, NEG)
            m = m_ref[j, :, cols]
            m_new = jnp.maximum(m, jnp.max(s_t, axis=0, keepdims=True))
            p = jnp.exp2(s_t - m_new).astype(BF16)
            acc_ref[j, :, cols] = jnp.exp2(m - m_new) * acc_ref[j, :, cols] + _dot(v, p)
            m_ref[j, :, cols] = m_new

    def pair(g):
        scores(2 * g + 1, nh - 1, s_odd)
        for j in range(nh):
            absorb(2 * g, j, s_even)
            scores(2 * g + 2, j, s_even)
        for j in range(nh):
            absorb(2 * g + 1, j, s_odd)
            if j < nh - 1:
                scores(2 * g + 3, j, s_odd)

    def run_pairs(start, count, per_step):
        def body(g, carry):
            for u in range(per_step):
                pair(start + per_step * g + u)
            return carry
        lax.fori_loop(0, count // per_step, body, 0)
        return start + per_step * (count // per_step), count % per_step

    for j in range(nh):
        scores(0, j, s_even)
    for j in range(nh - 1):
        scores(1, j, s_odd)
    m_ref[...] = jnp.full(m_ref.shape, NEG, F32)
    acc_ref[...] = jnp.zeros(acc_ref.shape, F32)
    nxt, left = run_pairs(0, qi * (n_diag // 2), 4)
    nxt, left = run_pairs(nxt, left, 2)
    if n_diag % 4:
        run_pairs(nxt, left, 1)
    first = qi * n_diag
    scores(first + 1, nh - 1, s_odd, tk)
    for d in range(n_diag):
        buf = s_odd if d % 2 else s_even
        for j in range(nh):
            absorb(first + d, j, buf, d * tk, diagonal=True)
            if d + 2 < n_diag:
                scores(first + d + 2, j, buf, (d + 2) * tk)
    outs = [acc_ref[j, 0:V_HEAD, :] / acc_ref[j, V_HEAD:V_HEAD + 1, :] for j in range(nh)]
    o_ref[...] = jnp.concatenate(outs, axis=0).T


def _mla_prompt(qt, k, vt, tq):
    b, _, s = qt.shape
    nkt, tk = vt.shape[1], vt.shape[3]
    nh = HEADS_PER_STEP
    return pl.pallas_call(
        _attn_body,
        grid=(b, N_HEADS // nh, s // tq),
        in_specs=[pl.BlockSpec((None, nh * HEAD_SLOT, tq), lambda bi, p, qi: (bi, p, qi)),
                  pl.BlockSpec((None, s, nh * HEAD_SLOT), lambda bi, p, qi: (bi, 0, p)),
                  pl.BlockSpec((None, nkt, nh * VT_ROWS, tk), lambda bi, p, qi: (bi, 0, p, 0))],
        out_specs=pl.BlockSpec((None, tq, nh * V_HEAD), lambda bi, p, qi: (bi, qi, p)),
        out_shape=jax.ShapeDtypeStruct((b, s, MLA_WIDTH), F32),
        scratch_shapes=[pltpu.VMEM((nh, tk, tq), F32), pltpu.VMEM((nh, tk, tq), F32),
                        pltpu.VMEM((nh, 1, tq), F32), pltpu.VMEM((nh, VT_ROWS, tq), F32)],
        compiler_params=pltpu.CompilerParams(
            dimension_semantics=("parallel", "parallel", "arbitrary")),
        name="mla_prompt",
    )(qt, k, vt)


def _memkv_body(mem_ref, g_ref, wk_ref, wv_ref, g64_ref, gmk_ref, k_ref, v_ref):
    hm = _rms(mem_ref[...], g_ref[...]).astype(BF16)
    k = _dot(hm, wk_ref[...])
    k_ref[...] = k * lax.rsqrt(_group_mean_sq(k, g64_ref) + EPS) * gmk_ref[...]
    v_ref[...] = _dot(hm, wv_ref[...])


def _mem_kv(mem, w):
    b, n, d = mem.shape
    consts = [w["g_mem"], w["w_mem_k"], w["w_mem_v"], w["G64"], w["gmk_vec"]]
    blk = pl.BlockSpec((None, n, MEM_WIDTH), lambda bi: (bi, 0, 0))
    return pl.pallas_call(
        _memkv_body,
        grid=(b,),
        in_specs=[pl.BlockSpec((None, n, d), lambda bi: (bi, 0, 0))]
        + [_const_spec(c.shape) for c in consts],
        out_specs=(blk, blk),
        out_shape=(jax.ShapeDtypeStruct((b, n, MEM_WIDTH), F32),) * 2,
        compiler_params=pltpu.CompilerParams(dimension_semantics=("parallel",)),
        name="mem_kv",
    )(mem, *consts)


def _out_proj(x, o_mla, y_conv, o_mem, g1_ref, g2_ref, g3_ref, wo_ref):
    o = _dot(_rms(o_mla, g1_ref[...]).astype(BF16), wo_ref[0:MLA_WIDTH, :])
    o += _dot(_rms(y_conv, g2_ref[...]).astype(BF16), wo_ref[MLA_WIDTH:MLA_WIDTH + CONV_WIDTH, :])
    o += _dot(_rms(o_mem, g3_ref[...]).astype(BF16), wo_ref[MLA_WIDTH + CONV_WIDTH:, :])
    return x + o


def _postmix_body(x_ref, omla_ref, u_ref, uprev_ref, gb_ref, mq_ref, mk_ref, mv_ref, cw_ref,
                  g1_ref, g2_ref, g3_ref, wo_ref, o_ref, ubuf):
    tm = x_ref.shape[0]
    n_mem = mv_ref.shape[0] // MEM_HEADS
    u = u_ref[...]
    ubuf[0:8, :] = jnp.where(pl.program_id(1) == 0, 0.0, uprev_ref[...])
    ubuf[8:8 + tm, :] = u
    cw = cw_ref[...]
    y = cw[0:1] * ubuf[pl.ds(6, tm), :] + cw[1:2] * ubuf[pl.ds(7, tm), :] + cw[2:3] * u
    y = gb_ref[...] * y
    s = _dot(mq_ref[...], mk_ref[...])
    ps = []
    for h in range(MEM_HEADS):
        sh = s[:, h * n_mem:(h + 1) * n_mem]
        p = jnp.exp2(sh - jnp.max(sh, axis=-1, keepdims=True))
        ps.append((p / jnp.sum(p, axis=-1, keepdims=True)).astype(BF16))
    o_mem = _dot(jnp.concatenate(ps, axis=1), mv_ref[...])
    o_ref[...] = _out_proj(x_ref[...], omla_ref[...], y, o_mem, g1_ref, g2_ref, g3_ref, wo_ref)


def _postmix(x, o_mla, u, gb, mq, mk_bd, mv_bd, w, tm):
    b, s, d = x.shape
    row = lambda width: pl.BlockSpec((None, tm, width), lambda bi, i: (bi, i, 0))
    prev = pl.BlockSpec((None, 8, CONV_WIDTH), lambda bi, i: (bi, jnp.maximum(i * (tm // 8) - 1, 0), 0))
    per_b = lambda a: pl.BlockSpec((None,) + a.shape[1:], lambda bi, i: (bi, 0, 0))
    consts = [w["conv_w"], w["g_out_mla"], w["g_out_conv"], w["g_out_mem"], w["w_o"]]
    return pl.pallas_call(
        _postmix_body,
        grid=(b, s // tm),
        in_specs=[row(d), row(MLA_WIDTH), row(CONV_WIDTH), prev, row(CONV_WIDTH), row(MEM_WIDTH),
                  per_b(mk_bd), per_b(mv_bd)] + [_const_spec(c.shape) for c in consts],
        out_specs=row(d),
        out_shape=jax.ShapeDtypeStruct((b, s, d), F32),
        scratch_shapes=[pltpu.VMEM((tm + 8, CONV_WIDTH), F32)],
        compiler_params=pltpu.CompilerParams(dimension_semantics=("parallel", "parallel")),
        name="postmix",
    )(x, o_mla, u, u, gb, mq, mk_bd, mv_bd, *consts)


def _absorb_body(q_ref, gk_ref, wt_ref, a_ref):
    for h in range(N_HEADS):
        qh = q_ref[:, h * HEAD_SLOT:(h + 1) * HEAD_SLOT].astype(F32) * gk_ref[...]
        a_ref[h] = _dot(qh.astype(BF16), wt_ref[h])


def _absorb(q, gk_slot, wuk_t):
    n = q.shape[0]
    return pl.pallas_call(
        _absorb_body,
        out_shape=jax.ShapeDtypeStruct((N_HEADS, n, KV_LORA), F32),
        name="absorb_q",
    )(q, gk_slot, wuk_t)


def _paged_body(pt_ref, wt_ref, a_ref, qr_ref, ckv_hbm, kr_hbm, acc_ref, m_ref, l_ref,
                xs, krs, lhs, s_scr, sem, *, n_chunks, ppc, page_base):
    nk = ppc * PAGE_SIZE
    nope = N_HEADS * QK_NOPE
    b = pl.program_id(0)
    last = pl.num_programs(0) * n_chunks - 1

    def copies(t, slot):
        out = []
        for i in range(ppc):
            pg = pt_ref[t * ppc + i] + page_base
            keys = pl.ds(i * PAGE_SIZE, PAGE_SIZE)
            out.append(pltpu.make_async_copy(ckv_hbm.at[pg], xs.at[slot, keys, :], sem.at[slot, 0]))
            out.append(pltpu.make_async_copy(kr_hbm.at[pg], krs.at[slot, :, keys], sem.at[slot, 1]))
        return out

    def start(t, slot):
        for cp in copies(t, slot):
            cp.start()

    def wait(t, slot):
        for cp in copies(t, slot):
            cp.wait()

    def scores(t, slot):
        e = t // n_chunks
        wait(t, slot)
        lhs[nope:, :] = jnp.concatenate([a_ref[e], jnp.zeros((8, KV_LORA), F32)], axis=0).astype(BF16)
        qr = jnp.concatenate([qr_ref[e], jnp.zeros((8, QK_ROPE), F32)], axis=0).astype(BF16)
        out = _dot_nt(lhs[...], xs[slot].astype(BF16))
        k_t = out[0:nope]
        n = jnp.sum((k_t * k_t).reshape(N_HEADS, QK_NOPE, nk), axis=1)
        rope = _dot(qr, krs[slot].astype(BF16))[0:N_HEADS]
        return out[nope:nope + N_HEADS] * lax.rsqrt(n * (1.0 / QK_NOPE) + EPS) + rope

    @pl.when(b == 0)
    def _():
        lhs[0:nope, :] = wt_ref[...]
        for t0 in range(PAGE_SLOTS - 1):
            start(t0, t0)
        s_scr[...] = scores(0, 0)

    def step(c, carry):
        m, l, acc, s = carry
        t = b * n_chunks + c
        s_next = scores(jnp.minimum(t + 1, last), (t + 1) % PAGE_SLOTS)
        m_new = jnp.maximum(m, jnp.max(s, axis=-1, keepdims=True))
        alpha = jnp.exp2(m - m_new)
        p = jnp.exp2(s - m_new)
        l = alpha * l + jnp.sum(p, axis=-1, keepdims=True)
        p16 = jnp.concatenate([p, jnp.zeros_like(p)], axis=0).astype(BF16)
        acc = alpha * acc + _dot(p16, xs[t % PAGE_SLOTS].astype(BF16))[0:N_HEADS]
        ahead = PAGE_SLOTS - 1
        start(jnp.minimum(t + ahead, last), (t + ahead) % PAGE_SLOTS)
        return m_new, l, acc, s_next

    init = (jnp.full((N_HEADS, 1), NEG, F32), jnp.zeros((N_HEADS, 1), F32),
            jnp.zeros((N_HEADS, KV_LORA), F32), s_scr[...])
    m, l, acc, s = lax.fori_loop(0, n_chunks, step, init)
    s_scr[...] = s
    acc_ref[0] = acc
    m_ref[0] = jnp.broadcast_to(m, (N_HEADS, HEAD_SLOT))
    l_ref[0] = jnp.broadcast_to(l, (N_HEADS, HEAD_SLOT))

    @pl.when(b == pl.num_programs(0) - 1)
    def _():
        for extra in range(2, PAGE_SLOTS):
            wait(last, (last + extra) % PAGE_SLOTS)


def _paged_mla(page_table, wuk_t2d, a, qr, cache_ckv, cache_krope, page_base):
    nb, n_pages = page_table.shape
    ppc = min(PAGES_PER_CHUNK, n_pages)
    n_chunks = n_pages // ppc
    assert n_chunks * ppc == n_pages and nb * n_chunks >= PAGE_SLOTS
    nk = ppc * PAGE_SIZE
    vmem = pl.BlockSpec(memory_space=pltpu.VMEM)
    per_b = pl.BlockSpec((1, N_HEADS, HEAD_SLOT), lambda bi: (bi, 0, 0))
    out = jax.ShapeDtypeStruct((nb, N_HEADS, HEAD_SLOT), F32)
    return pl.pallas_call(
        functools.partial(_paged_body, n_chunks=n_chunks, ppc=ppc, page_base=page_base),
        grid=(nb,),
        in_specs=[pl.BlockSpec(memory_space=pltpu.SMEM), vmem, vmem, vmem,
                  pl.BlockSpec(memory_space=pl.ANY), pl.BlockSpec(memory_space=pl.ANY)],
        out_specs=(per_b, per_b, per_b),
        out_shape=(out, out, out),
        scratch_shapes=[pltpu.VMEM((PAGE_SLOTS, nk, KV_LORA), F32),
                        pltpu.VMEM((PAGE_SLOTS, QK_ROPE, nk), F32),
                        pltpu.VMEM((N_HEADS * QK_NOPE + 16, KV_LORA), BF16),
                        pltpu.VMEM((N_HEADS, nk), F32),
                        pltpu.SemaphoreType.DMA((PAGE_SLOTS, 2))],
        compiler_params=pltpu.CompilerParams(dimension_semantics=("arbitrary",)),
        name="paged_mla",
    )(page_table.reshape(-1), wuk_t2d, a, qr, cache_ckv, cache_krope)


def _smem_body(mq_ref, kt_ref, vt_ref, o_ref):
    bc, w, n_mem = kt_ref.shape
    step = pl.program_id(0)
    lane = lax.broadcasted_iota(jnp.int32, o_ref.shape, 1)

    @pl.when(step == 0)
    def _():
        o_ref[...] = jnp.zeros_like(o_ref)

    out = o_ref[...]
    for i in range(bc):
        q = jnp.concatenate([mq_ref[i]] * (n_mem // HEAD_SLOT), axis=1)
        s = jnp.sum((kt_ref[i] * q).reshape(MEM_HEADS, MEM_HEAD_DIM, n_mem), axis=1)
        p = jnp.exp2(s - jnp.max(s, axis=-1, keepdims=True))
        p = p / jnp.sum(p, axis=-1, keepdims=True)
        col = jnp.concatenate(
            [jnp.sum(vt_ref[i, h * MEM_HEAD_DIM:(h + 1) * MEM_HEAD_DIM, :] * p[h:h + 1, :],
                     axis=-1, keepdims=True) for h in range(MEM_HEADS)], axis=0)
        out = jnp.where(lane == step * bc + i, col, out)
    o_ref[...] = out


def _sample_mem_attend(mq_lanes, kt, vt, bc=8):
    nb, w, n_mem = kt.shape
    return pl.pallas_call(
        _smem_body,
        grid=(nb // bc,),
        in_specs=[pl.BlockSpec((bc, w, HEAD_SLOT), lambda i: (i, 0, 0)),
                  pl.BlockSpec((bc, w, n_mem), lambda i: (i, 0, 0)),
                  pl.BlockSpec((bc, w, n_mem), lambda i: (i, 0, 0))],
        out_specs=pl.BlockSpec((w, nb), lambda i: (0, 0)),
        out_shape=jax.ShapeDtypeStruct((w, nb), F32),
        compiler_params=pltpu.CompilerParams(dimension_semantics=("arbitrary",)),
        name="sample_mem_attend",
    )(mq_lanes, kt, vt)


def _spost_body(x_ref, q_ref, k_ref, ckv_ref, acc_ref, m_ref, l_ref, wuv_ref, u_ref, gb_ref,
                s0_ref, s1_ref, cw_ref, omem_ref, g1_ref, g2_ref, g3_ref, wo_ref, o_ref):
    ckv = ckv_ref[...]
    o_mla = jnp.zeros((x_ref.shape[0], MLA_WIDTH), F32)
    for h in range(N_HEADS):
        sl = slice(h * HEAD_SLOT, (h + 1) * HEAD_SLOT)
        s_new = jnp.sum(q_ref[:, sl].astype(F32) * k_ref[:, sl].astype(F32), axis=-1, keepdims=True)
        m_old = m_ref[h][:, 0:1]
        l_old = l_ref[h][:, 0:1]
        m_new = jnp.maximum(m_old, s_new)
        alpha = jnp.exp2(m_old - m_new)
        p_new = jnp.exp2(s_new - m_new)
        o_lat = (acc_ref[h] * alpha + p_new * ckv) / (l_old * alpha + p_new)
        o_mla += _dot(o_lat.astype(BF16), wuv_ref[h])
    cw = cw_ref[...]
    y = gb_ref[...] * (cw[0:1] * s0_ref[...] + cw[1:2] * s1_ref[...] + cw[2:3] * u_ref[...])
    o_ref[...] = _out_proj(x_ref[...], o_mla, y, omem_ref[...], g1_ref, g2_ref, g3_ref, wo_ref)


def _sample_postmix(*args):
    n = args[0].shape[0]
    return pl.pallas_call(
        _spost_body,
        out_shape=jax.ShapeDtypeStruct((n, D_MODEL), F32),
        name="sample_postmix",
    )(*args)


def _rope_tables(pos):
    inv_freq = ROPE_THETA ** (-jnp.arange(0, QK_ROPE, 2, dtype=F32) / QK_ROPE)
    ang = pos.astype(F32)[:, None] * inv_freq[None, :]
    cos, sin = jnp.cos(ang), jnp.sin(ang)
    n = pos.shape[0]
    zeros = lambda w_: jnp.zeros((n, w_), F32)
    tail = HEAD_SLOT - ROPE_LO - QK_ROPE
    c = jnp.concatenate([jnp.ones((n, ROPE_LO), F32), cos, cos, zeros(tail)], axis=1)
    s1 = jnp.concatenate([zeros(ROPE_LO), -sin, zeros(HALF_ROPE), zeros(tail)], axis=1)
    s2 = jnp.concatenate([zeros(ROPE_LO), zeros(HALF_ROPE), sin, zeros(tail)], axis=1)
    return c, s1, s2


def _block_diag_avg(sizes, width):
    idx = jnp.arange(width)
    gid = jnp.full((width,), -1, jnp.int32)
    scale = jnp.zeros((width,), F32)
    lo = 0
    for g, sz in enumerate(sizes):
        inside = (idx >= lo) & (idx < lo + sz)
        gid = jnp.where(inside, g, gid)
        scale = jnp.where(inside, 1.0 / sz, scale)
        lo += sz
    same = (gid[:, None] == gid[None, :]) & (gid[:, None] >= 0)
    return jnp.where(same, scale[None, :], 0.0).astype(BF16)


def _layer_weights(l, p):
    pad_last = lambda a, n: jnp.pad(a, [(0, 0)] * (a.ndim - 1) + [(0, n - a.shape[-1])])
    row = lambda v: v.reshape(1, -1).astype(F32)
    w_in = p["w_in"][l]
    zc = lambda n: jnp.zeros((D_MODEL, n), F32)
    w_in_p = jnp.concatenate([w_in[:, 0:384], zc(ROPE_LO), w_in[:, 384:416],
                              zc(HEAD_SLOT - ROPE_LO - QK_ROPE), w_in[:, 416:]], axis=1)
    slot_gain = lambda nope, rope: jnp.concatenate(
        [nope, rope, jnp.zeros((HEAD_SLOT - QK_NOPE - QK_ROPE,), F32)])
    zeros_n = jnp.zeros((QK_NOPE,), F32)
    zeros_r = jnp.zeros((QK_ROPE,), F32)
    g_kn = p["g_kn"][l]
    w_uk = p["w_uk"][l]
    g_slot = _block_diag_avg((QK_NOPE, QK_ROPE), HEAD_SLOT).astype(F32)
    g2 = jnp.kron(jnp.eye(2, dtype=F32), g_slot).astype(BF16)
    wuv = p["w_uv"][l]
    wuv_bd = jnp.stack([jnp.pad(wuv[:, h, :], ((0, 0), (h * V_HEAD, MLA_WIDTH - (h + 1) * V_HEAD)))
                        for h in range(N_HEADS)])
    return {
        "g_ffn1": row(p["g_ffn1"][l]), "w1_gate": p["w1_gate"][l].astype(BF16),
        "w1_up": p["w1_up"][l].astype(BF16), "w1_down": p["w1_down"][l].astype(BF16),
        "g_ffn2": row(p["g_ffn2"][l]), "w2_gate": p["w2_gate"][l].astype(BF16),
        "w2_up": p["w2_up"][l].astype(BF16), "w2_down": p["w2_down"][l].astype(BF16),
        "g_mix": row(p["g_mix"][l]), "w_in": w_in_p.astype(BF16),
        "g_q_lora": row(p["g_q_lora"][l]),
        "w_uq": pad_last(p["w_uq"][l].reshape(Q_LORA, N_HEADS, QK_NOPE + QK_ROPE), HEAD_SLOT)
        .reshape(Q_LORA, N_HEADS * HEAD_SLOT).astype(BF16),
        "gq_vec": row(jnp.tile(slot_gain(p["g_qn"][l], p["g_qr"][l]), N_HEADS) * (MLA_SCALE * LOG2E)),
        "G2": g2,
        "g_kv_lora": row(p["g_kv_lora"][l]),
        "gkr_vec": row(slot_gain(zeros_n, p["g_kr"][l])),
        "w_uk": pad_last(w_uk, HEAD_SLOT).reshape(KV_LORA, N_HEADS * HEAD_SLOT).astype(BF16),
        "gk_vec": row(jnp.tile(slot_gain(g_kn, zeros_r), N_HEADS)),
        "gk_slot": row(slot_gain(g_kn, zeros_r)),
        "w_uv": wuv.reshape(KV_LORA, MLA_WIDTH).astype(BF16),
        "wuv_bd": wuv_bd.astype(BF16),
        "wuk_t_pad": jnp.pad(jnp.transpose(w_uk, (1, 2, 0)), ((0, 0), (0, HEAD_SLOT - QK_NOPE), (0, 0)))
        .astype(BF16),
        "wuk_t2d": jnp.transpose(w_uk, (1, 2, 0)).reshape(N_HEADS * QK_NOPE, KV_LORA).astype(BF16),
        "gmq_vec": row(jnp.tile(p["g_mq"][l], MEM_HEADS) * (MEM_SCALE * LOG2E)),
        "G64": _block_diag_avg((MEM_HEAD_DIM,) * MEM_HEADS, MEM_WIDTH),
        "g_mem": row(p["g_mem"][l]), "w_mem_k": p["w_mem_k"][l].astype(BF16),
        "w_mem_v": p["w_mem_v"][l].astype(BF16),
        "gmk_vec": row(jnp.tile(p["g_mk"][l], MEM_HEADS)),
        "conv_w": p["conv_w"][l].astype(F32),
        "g_out_mla": row(p["g_out_mla"][l]), "g_out_conv": row(p["g_out_conv"][l]),
        "g_out_mem": row(p["g_out_mem"][l]), "w_o": p["w_o"][l].astype(BF16),
    }


def _mem_block_diag(mk, mv):
    b, n, _ = mk.shape
    k4 = mk.reshape(b, n, MEM_HEADS, MEM_HEAD_DIM)
    v4 = mv.reshape(b, n, MEM_HEADS, MEM_HEAD_DIM)
    eye = jnp.eye(MEM_HEADS, dtype=F32)
    k_bd = jnp.einsum("bnhd,hg->bhdgn", k4, eye).reshape(b, MEM_WIDTH, MEM_HEADS * n)
    v_bd = jnp.einsum("bnhd,hg->bhngd", v4, eye).reshape(b, MEM_HEADS * n, MEM_WIDTH)
    return k_bd.astype(BF16), v_bd.astype(BF16)


def _prompt_layer(x, mem, w, tables, tm):
    b, s, d = x.shape
    x = _ffn_half(x.reshape(b * s, d), w["g_ffn1"], w["w1_gate"], w["w1_up"], w["w1_down"],
                  tm).reshape(b, s, d)
    qt, k, vt, ckv, kr, u, gb, mq = _premix(x, tables, w, tm)
    o_mla = _mla_prompt(qt, k, vt, min(QUERY_TILE, s))
    mk, mv = _mem_kv(mem, w)
    mk_bd, mv_bd = _mem_block_diag(mk, mv)
    x = _postmix(x, o_mla, u, gb, mq, mk_bd, mv_bd, w, tm)
    x = _ffn_half(x.reshape(b * s, d), w["g_ffn2"], w["w2_gate"], w["w2_up"], w["w2_down"],
                  tm).reshape(b, s, d)
    return x, ckv, jnp.swapaxes(kr, 1, 2), u, mk, mv


def _sample_layer(x, w, tables, cache_ckv, cache_krope, page_base, page_table, state, mem_k, mem_v):
    n = x.shape[0]
    x = _ffn_half(x, w["g_ffn1"], w["w1_gate"], w["w1_up"], w["w1_down"], n)
    qt, k, _, ckv, kr, u, gb, mq = _premix(x[None], tables, w, n)
    q = qt[0].T
    a = _absorb(q, w["gk_slot"], w["wuk_t_pad"])
    qr = q.reshape(n, N_HEADS, HEAD_SLOT)[:, :, ROPE_LO:ROPE_LO + QK_ROPE].astype(F32)
    acc, m, l = _paged_mla(page_table, w["wuk_t2d"], jnp.transpose(a, (1, 0, 2)), qr,
                           cache_ckv, cache_krope, page_base)
    head_major = lambda t: jnp.transpose(t, (1, 0, 2))
    feature_major = lambda t: jnp.transpose(t, (0, 2, 3, 1)).reshape(n, MEM_WIDTH, -1)
    mq_lanes = jnp.broadcast_to(mq[0].astype(F32)[:, :, None], (n, MEM_WIDTH, HEAD_SLOT))
    o_mem = _sample_mem_attend(mq_lanes, feature_major(mem_k), feature_major(mem_v)).T
    x = _sample_postmix(x, q, k[0], ckv[0], head_major(acc), head_major(m), head_major(l),
                        w["wuv_bd"], u[0], gb[0], state[:, 0, :], state[:, 1, :], w["conv_w"], o_mem,
                        w["g_out_mla"], w["g_out_conv"], w["g_out_mem"], w["w_o"])
    x = _ffn_half(x, w["g_ffn2"], w["w2_gate"], w["w2_up"], w["w2_down"], n)
    return x, ckv[0], kr[0].T, jnp.stack([state[:, 1, :], u[0]], axis=1)


def kernel(x_prompt, mem_prompt, x_sample, cache_ckv, cache_krope, page_table, state_conv, cache_mem_k,
           cache_mem_v, g_ffn1, w1_gate, w1_up, w1_down, g_mix, w_in, g_q_lora, w_uq, g_qn, g_qr,
           g_kv_lora, w_uk, w_uv, g_kn, g_kr, conv_w, g_mem, w_mem_k, w_mem_v, g_mq, g_mk, g_out_mla,
           g_out_conv, g_out_mem, w_o, g_ffn2, w2_gate, w2_up, w2_down):
    params = dict(g_ffn1=g_ffn1, w1_gate=w1_gate, w1_up=w1_up, w1_down=w1_down, g_mix=g_mix, w_in=w_in,
                  g_q_lora=g_q_lora, w_uq=w_uq, g_qn=g_qn, g_qr=g_qr, g_kv_lora=g_kv_lora, w_uk=w_uk,
                  w_uv=w_uv, g_kn=g_kn, g_kr=g_kr, conv_w=conv_w, g_mem=g_mem, w_mem_k=w_mem_k,
                  w_mem_v=w_mem_v, g_mq=g_mq, g_mk=g_mk, g_out_mla=g_out_mla, g_out_conv=g_out_conv,
                  g_out_mem=g_out_mem, w_o=w_o, g_ffn2=g_ffn2, w2_gate=w2_gate, w2_up=w2_up,
                  w2_down=w2_down)
    depth = w_in.shape[0]
    b, s, _ = x_prompt.shape
    nb, dec_seq, _ = x_sample.shape
    assert dec_seq == 1
    n_phys = cache_ckv.shape[1]
    tm = min(ROW_TILE, s)
    tab_p = _rope_tables(jnp.arange(s))
    tab_s = _rope_tables(jnp.full((nb,), PAST_LEN, jnp.int32))
    ckv_pages = cache_ckv.reshape(depth * n_phys, PAGE_SIZE, KV_LORA)
    kr_pages = jnp.swapaxes(cache_krope, 2, 3).reshape(depth * n_phys, QK_ROPE, PAGE_SIZE)

    xp, xs = x_prompt, x_sample.reshape(nb, D_MODEL)
    outs_p, outs_s = [], []
    for l in range(depth):
        w = _layer_weights(l, params)
        xp, ckv, kr, u, mk, mv = _prompt_layer(xp, mem_prompt, w, tab_p, tm)
        outs_p.append((ckv, kr, u[:, -(CONV_K - 1):], mk.reshape(b, -1, MEM_HEADS, MEM_HEAD_DIM),
                       mv.reshape(b, -1, MEM_HEADS, MEM_HEAD_DIM)))
        xs, ckv_s, kr_s, conv_s = _sample_layer(xs, w, tab_s, ckv_pages, kr_pages, l * n_phys, page_table,
                                                state_conv[l], cache_mem_k[l], cache_mem_v[l])
        outs_s.append((ckv_s[:, None, :], kr_s[:, None, :], conv_s))
    stack = lambda items, i: jnp.stack([it[i] for it in items])
    return (xp, xs.reshape(nb, 1, D_MODEL), stack(outs_p, 0), stack(outs_p, 1), stack(outs_p, 2),
            stack(outs_p, 3), stack(outs_p, 4), stack(outs_s, 0), stack(outs_s, 1), stack(outs_s, 2))
```

```python
import functools

import jax
import jax.numpy as jnp
from jax import lax
from jax.experimental import pallas as pl
from jax.experimental.pallas import tpu as pltpu

F32 = jnp.float32
BF16 = jnp.bfloat16

D_MODEL = 1024
N_HEADS = 8
Q_LORA = 256
KV_LORA = 128
QK_NOPE = 64
QK_ROPE = 32
V_HEAD = 64
MLA_WIDTH = N_HEADS * V_HEAD
CONV_WIDTH = 256
CONV_K = 3
MEM_HEADS = 4
MEM_HEAD_DIM = 64
MEM_WIDTH = MEM_HEADS * MEM_HEAD_DIM
D_FF = 2816
ROPE_THETA = 10000.0
EPS = 1e-6
PAST_LEN = 16384
PAGE_SIZE = 128
MLA_SCALE = (QK_NOPE + QK_ROPE) ** -0.5
MEM_SCALE = MEM_HEAD_DIM ** -0.5
LOG2E = 1.4426950408889634

HEAD_SLOT = 128
ROPE_LO = QK_NOPE
HALF_ROPE = QK_ROPE // 2
Z_WIDTH = 1536
VT_ROWS = 80
NEG = -1e30
PAGES_PER_CHUNK = 64
PAGE_SLOTS = 4
ROW_TILE = 512
KEY_SUB = 256
QUERY_TILE = 1024
HEADS_PER_STEP = 4


def _rms(x, g):
    ms = jnp.mean(x * x, axis=-1, keepdims=True)
    return x * lax.rsqrt(ms + EPS) * g


def _dot(a, b):
    return jnp.dot(a, b, preferred_element_type=F32)


def _dot_nt(a, b):
    return lax.dot_general(a, b, (((1,), (1,)), ((), ())), preferred_element_type=F32)


def _group_mean_sq(x, g_ref):
    x2 = (x * x).astype(BF16)
    g = g_ref[...]
    cols = [_dot(x2[:, j * 256:(j + 1) * 256], g) for j in range(x.shape[1] // 256)]
    return cols[0] if len(cols) == 1 else jnp.concatenate(cols, axis=1)


def _rope_slab(v, cos, s1, s2):
    return (v * cos + pltpu.roll(v, HEAD_SLOT - HALF_ROPE, 1) * s1
            + pltpu.roll(v, HALF_ROPE, 1) * s2)


def _const_spec(shape):
    nd = len(shape)
    return pl.BlockSpec(shape, lambda *_: (0,) * nd, pipeline_mode=pl.Buffered(1))


def _swiglu_half_step(x, g_ref, wg_ref, wu_ref, wd_ref):
    xn = _rms(x, g_ref[...]).astype(BF16)
    h = _dot(xn, wg_ref[...])
    u = _dot(xn, wu_ref[...])
    a = (h / (1.0 + jnp.exp(-h)) * u).astype(BF16)
    return x + 0.5 * _dot(a, wd_ref[...])


def _ffn_body(x_ref, g_ref, wg_ref, wu_ref, wd_ref, o_ref):
    o_ref[...] = _swiglu_half_step(x_ref[...], g_ref, wg_ref, wu_ref, wd_ref)


def _ffn_half(x, g, wg, wu, wd, tm):
    n, d = x.shape
    dff = wg.shape[1]
    return pl.pallas_call(
        _ffn_body,
        grid=(n // tm,),
        in_specs=[pl.BlockSpec((tm, d), lambda i: (i, 0)),
                  _const_spec((1, d)), _const_spec((d, dff)), _const_spec((d, dff)),
                  _const_spec((dff, d))],
        out_specs=pl.BlockSpec((tm, d), lambda i: (i, 0)),
        out_shape=jax.ShapeDtypeStruct((n, d), F32),
        compiler_params=pltpu.CompilerParams(dimension_semantics=("parallel",)),
        name="ffn_half",
    )(x, g, wg, wu, wd)


def _premix_body(x_ref, cos_ref, s1_ref, s2_ref, gmix_ref, win_ref, gql_ref, wuq_ref, gq_ref,
                 g2_ref, gkvl_ref, gkr_ref, wuk_ref, gk_ref, wuv_ref, gmq_ref, g64_ref,
                 qt_ref, k_ref, vt_ref, ckv_ref, kr_ref, u_ref, gb_ref, mq_ref):
    tm = x_ref.shape[0]
    hn = _rms(x_ref[...], gmix_ref[...]).astype(BF16)
    z = _dot(hn, win_ref[...])
    c_q, c_kv, k_r = z[:, 0:256], z[:, 256:384], z[:, 384:512]
    u_in, g_b, g_c, m_q = z[:, 512:768], z[:, 768:1024], z[:, 1024:1280], z[:, 1280:1536]
    cos, s1, s2 = cos_ref[...], s1_ref[...], s2_ref[...]

    q = _dot(_rms(c_q, gql_ref[...]).astype(BF16), wuq_ref[...])
    q = q * lax.rsqrt(_group_mean_sq(q, g2_ref) + EPS) * gq_ref[...]
    q = jnp.concatenate(
        [_rope_slab(q[:, h * HEAD_SLOT:(h + 1) * HEAD_SLOT], cos, s1, s2) for h in range(N_HEADS)],
        axis=1)
    qt_ref[...] = q.T.astype(BF16)

    ckv = _rms(c_kv, gkvl_ref[...])
    ckv_ref[...] = ckv
    ckv16 = ckv.astype(BF16)
    kr_ms = jnp.sum(k_r * k_r, axis=-1, keepdims=True) * (1.0 / QK_ROPE)
    krr = _rope_slab(k_r * lax.rsqrt(kr_ms + EPS) * gkr_ref[...], cos, s1, s2)
    kr_ref[...] = krr.T[ROPE_LO:ROPE_LO + QK_ROPE, :]
    kk = _dot(ckv16, wuk_ref[...])
    kk = kk * lax.rsqrt(_group_mean_sq(kk, g2_ref) + EPS) * gk_ref[...]
    k_ref[...] = jnp.concatenate(
        [kk[:, h * HEAD_SLOT:(h + 1) * HEAD_SLOT] + krr for h in range(N_HEADS)],
        axis=1).astype(BF16)
    vt = _dot(ckv16, wuv_ref[...]).T
    tkv = vt_ref.shape[2]
    ones_rows = jnp.where(lax.broadcasted_iota(jnp.int32, (VT_ROWS - V_HEAD, tkv), 0) == 0,
                          1.0, 0.0).astype(BF16)
    for u in range(tm // tkv):
        for h in range(N_HEADS):
            vt_ref[u, h * VT_ROWS:h * VT_ROWS + V_HEAD, :] = (
                vt[h * V_HEAD:(h + 1) * V_HEAD, u * tkv:(u + 1) * tkv].astype(BF16))
            vt_ref[u, h * VT_ROWS + V_HEAD:(h + 1) * VT_ROWS, :] = ones_rows

    u_ref[...] = g_c * u_in
    gb_ref[...] = g_b
    mq = m_q * lax.rsqrt(_group_mean_sq(m_q, g64_ref) + EPS) * gmq_ref[...]
    mq_ref[...] = mq.astype(BF16)


def _premix(x, tables, w, tm):
    b, s, d = x.shape
    tkv = min(KEY_SUB, tm)
    cos, s1, s2 = tables
    row = lambda width: pl.BlockSpec((None, tm, width), lambda bi, i: (bi, i, 0))
    tab = pl.BlockSpec((tm, HEAD_SLOT), lambda bi, i: (i, 0))
    consts = [w["g_mix"], w["w_in"], w["g_q_lora"], w["w_uq"], w["gq_vec"], w["G2"], w["g_kv_lora"],
              w["gkr_vec"], w["w_uk"], w["gk_vec"], w["w_uv"], w["gmq_vec"], w["G64"]]
    out_shape = (
        jax.ShapeDtypeStruct((b, N_HEADS * HEAD_SLOT, s), BF16),
        jax.ShapeDtypeStruct((b, s, N_HEADS * HEAD_SLOT), BF16),
        jax.ShapeDtypeStruct((b, s // tkv, N_HEADS * VT_ROWS, tkv), BF16),
        jax.ShapeDtypeStruct((b, s, KV_LORA), F32),
        jax.ShapeDtypeStruct((b, QK_ROPE, s), F32),
        jax.ShapeDtypeStruct((b, s, CONV_WIDTH), F32),
        jax.ShapeDtypeStruct((b, s, CONV_WIDTH), F32),
        jax.ShapeDtypeStruct((b, s, MEM_WIDTH), BF16),
    )
    out_specs = (
        pl.BlockSpec((None, N_HEADS * HEAD_SLOT, tm), lambda bi, i: (bi, 0, i)),
        row(N_HEADS * HEAD_SLOT),
        pl.BlockSpec((None, tm // tkv, N_HEADS * VT_ROWS, tkv), lambda bi, i: (bi, i, 0, 0)),
        row(KV_LORA), pl.BlockSpec((None, QK_ROPE, tm), lambda bi, i: (bi, 0, i)),
        row(CONV_WIDTH), row(CONV_WIDTH), row(MEM_WIDTH),
    )
    return pl.pallas_call(
        _premix_body,
        grid=(b, s // tm),
        in_specs=[row(d), tab, tab, tab] + [_const_spec(c.shape) for c in consts],
        out_specs=out_specs,
        out_shape=out_shape,
        compiler_params=pltpu.CompilerParams(dimension_semantics=("parallel", "parallel")),
        name="premix",
    )(x, cos, s1, s2, *consts)


def _attn_body(qt_ref, k_ref, vt_ref, o_ref, s_even, s_odd, m_ref, acc_ref):
    tq = qt_ref.shape[1]
    tk = vt_ref.shape[2]
    nh = qt_ref.shape[0] // HEAD_SLOT
    n_diag = tq // tk
    assert n_diag * tk == tq and n_diag % 2 == 0
    qi = pl.program_id(2)
    q_t = [qt_ref[j * HEAD_SLOT:(j + 1) * HEAD_SLOT, :] for j in range(nh)]

    def scores(i, j, s_ref, lo=0):
        rows = pl.ds(pl.multiple_of(i * tk, tk), tk)
        s_ref[j, :, lo:] = _dot(k_ref[rows, j * HEAD_SLOT:(j + 1) * HEAD_SLOT], q_t[j][:, lo:])

    def absorb(i, j, s_ref, lo=0, diagonal=False):
        v = vt_ref[i, j * VT_ROWS:(j + 1) * VT_ROWS, :]
        for c in range(lo // tk, n_diag):
            cols = slice(c * tk, (c + 1) * tk)
            s_t = s_ref[j, :, cols]
            if diagonal and c * tk == lo:
                kpos = lax.broadcasted_iota(jnp.int32, s_t.shape, 0)
                qpos = lax.broadcasted_iota(jnp.int32, s_t.shape, 1)
                s_t = jnp.where(kpos <= qpos, s_t, NEG)
            m = m_ref[j, :, cols]
            m_new = jnp.maximum(m, jnp.max(s_t, axis=0, keepdims=True))
            p = jnp.exp2(s_t - m_new).astype(BF16)
            acc_ref[j, :, cols] = jnp.exp2(m - m_new) * acc_ref[j, :, cols] + _dot(v, p)
            m_ref[j, :, cols] = m_new

    def pair(g):
        scores(2 * g + 1, nh - 1, s_odd)
        for j in range(nh):
            absorb(2 * g, j, s_even)
            scores(2 * g + 2, j, s_even)
        for j in range(nh):
            absorb(2 * g + 1, j, s_odd)
            if j < nh - 1:
                scores(2 * g + 3, j, s_odd)

    def run_pairs(start, count, per_step):
        def body(g, carry):
            for u in range(per_step):
                pair(start + per_step * g + u)
            return carry
        lax.fori_loop(0, count // per_step, body, 0)
        return start + per_step * (count // per_step), count % per_step

    for j in range(nh):
        scores(0, j, s_even)
    for j in range(nh - 1):
        scores(1, j, s_odd)
    m_ref[...] = jnp.full(m_ref.shape, NEG, F32)
    acc_ref[...] = jnp.zeros(acc_ref.shape, F32)
    nxt, left = run_pairs(0, qi * (n_diag // 2), 4)
    nxt, left = run_pairs(nxt, left, 2)
    if n_diag % 4:
        run_pairs(nxt, left, 1)
    first = qi * n_diag
    scores(first + 1, nh - 1, s_odd, tk)
    for d in range(n_diag):
        buf = s_odd if d % 2 else s_even
        for j in range(nh):
            absorb(first + d, j, buf, d * tk, diagonal=True)
            if d + 2 < n_diag:
                scores(first + d + 2, j, buf, (d + 2) * tk)
    outs = [acc_ref[j, 0:V_HEAD, :] / acc_ref[j, V_HEAD:V_HEAD + 1, :] for j in range(nh)]
    o_ref[...] = jnp.concatenate(outs, axis=0).T


def _mla_prompt(qt, k, vt, tq):
    b, _, s = qt.shape
    nkt, tk = vt.shape[1], vt.shape[3]
    nh = HEADS_PER_STEP
    return pl.pallas_call(
        _attn_body,
        grid=(b, N_HEADS // nh, s // tq),
        in_specs=[pl.BlockSpec((None, nh * HEAD_SLOT, tq), lambda bi, p, qi: (bi, p, qi)),
                  pl.BlockSpec((None, s, nh * HEAD_SLOT), lambda bi, p, qi: (bi, 0, p)),
                  pl.BlockSpec((None, nkt, nh * VT_ROWS, tk), lambda bi, p, qi: (bi, 0, p, 0))],
        out_specs=pl.BlockSpec((None, tq, nh * V_HEAD), lambda bi, p, qi: (bi, qi, p)),
        out_shape=jax.ShapeDtypeStruct((b, s, MLA_WIDTH), F32),
        scratch_shapes=[pltpu.VMEM((nh, tk, tq), F32), pltpu.VMEM((nh, tk, tq), F32),
                        pltpu.VMEM((nh, 1, tq), F32), pltpu.VMEM((nh, VT_ROWS, tq), F32)],
        compiler_params=pltpu.CompilerParams(
            dimension_semantics=("parallel", "parallel", "arbitrary")),
        name="mla_prompt",
    )(qt, k, vt)


def _memkv_body(mem_ref, g_ref, wk_ref, wv_ref, g64_ref, gmk_ref, k_ref, v_ref):
    hm = _rms(mem_ref[...], g_ref[...]).astype(BF16)
    k = _dot(hm, wk_ref[...])
    k_ref[...] = k * lax.rsqrt(_group_mean_sq(k, g64_ref) + EPS) * gmk_ref[...]
    v_ref[...] = _dot(hm, wv_ref[...])


def _mem_kv(mem, w):
    b, n, d = mem.shape
    consts = [w["g_mem"], w["w_mem_k"], w["w_mem_v"], w["G64"], w["gmk_vec"]]
    blk = pl.BlockSpec((None, n, MEM_WIDTH), lambda bi: (bi, 0, 0))
    return pl.pallas_call(
        _memkv_body,
        grid=(b,),
        in_specs=[pl.BlockSpec((None, n, d), lambda bi: (bi, 0, 0))]
        + [_const_spec(c.shape) for c in consts],
        out_specs=(blk, blk),
        out_shape=(jax.ShapeDtypeStruct((b, n, MEM_WIDTH), F32),) * 2,
        compiler_params=pltpu.CompilerParams(dimension_semantics=("parallel",)),
        name="mem_kv",
    )(mem, *consts)


def _out_proj(x, o_mla, y_conv, o_mem, g1_ref, g2_ref, g3_ref, wo_ref):
    o = _dot(_rms(o_mla, g1_ref[...]).astype(BF16), wo_ref[0:MLA_WIDTH, :])
    o += _dot(_rms(y_conv, g2_ref[...]).astype(BF16), wo_ref[MLA_WIDTH:MLA_WIDTH + CONV_WIDTH, :])
    o += _dot(_rms(o_mem, g3_ref[...]).astype(BF16), wo_ref[MLA_WIDTH + CONV_WIDTH:, :])
    return x + o


def _postmix_body(x_ref, omla_ref, u_ref, uprev_ref, gb_ref, mq_ref, mk_ref, mv_ref, cw_ref,
                  g1_ref, g2_ref, g3_ref, wo_ref, o_ref, ubuf):
    tm = x_ref.shape[0]
    n_mem = mv_ref.shape[0] // MEM_HEADS
    u = u_ref[...]
    ubuf[0:8, :] = jnp.where(pl.program_id(1) == 0, 0.0, uprev_ref[...])
    ubuf[8:8 + tm, :] = u
    cw = cw_ref[...]
    y = cw[0:1] * ubuf[pl.ds(6, tm), :] + cw[1:2] * ubuf[pl.ds(7, tm), :] + cw[2:3] * u
    y = gb_ref[...] * y
    s = _dot(mq_ref[...], mk_ref[...])
    ps = []
    for h in range(MEM_HEADS):
        sh = s[:, h * n_mem:(h + 1) * n_mem]
        p = jnp.exp2(sh - jnp.max(sh, axis=-1, keepdims=True))
        ps.append((p / jnp.sum(p, axis=-1, keepdims=True)).astype(BF16))
    o_mem = _dot(jnp.concatenate(ps, axis=1), mv_ref[...])
    o_ref[...] = _out_proj(x_ref[...], omla_ref[...], y, o_mem, g1_ref, g2_ref, g3_ref, wo_ref)


def _postmix(x, o_mla, u, gb, mq, mk_bd, mv_bd, w, tm):
    b, s, d = x.shape
    row = lambda width: pl.BlockSpec((None, tm, width), lambda bi, i: (bi, i, 0))
    prev = pl.BlockSpec((None, 8, CONV_WIDTH), lambda bi, i: (bi, jnp.maximum(i * (tm // 8) - 1, 0), 0))
    per_b = lambda a: pl.BlockSpec((None,) + a.shape[1:], lambda bi, i: (bi, 0, 0))
    consts = [w["conv_w"], w["g_out_mla"], w["g_out_conv"], w["g_out_mem"], w["w_o"]]
    return pl.pallas_call(
        _postmix_body,
        grid=(b, s // tm),
        in_specs=[row(d), row(MLA_WIDTH), row(CONV_WIDTH), prev, row(CONV_WIDTH), row(MEM_WIDTH),
                  per_b(mk_bd), per_b(mv_bd)] + [_const_spec(c.shape) for c in consts],
        out_specs=row(d),
        out_shape=jax.ShapeDtypeStruct((b, s, d), F32),
        scratch_shapes=[pltpu.VMEM((tm + 8, CONV_WIDTH), F32)],
        compiler_params=pltpu.CompilerParams(dimension_semantics=("parallel", "parallel")),
        name="postmix",
    )(x, o_mla, u, u, gb, mq, mk_bd, mv_bd, *consts)


def _absorb_body(q_ref, gk_ref, wt_ref, a_ref):
    for h in range(N_HEADS):
        qh = q_ref[:, h * HEAD_SLOT:(h + 1) * HEAD_SLOT].astype(F32) * gk_ref[...]
        a_ref[h] = _dot(qh.astype(BF16), wt_ref[h])


def _absorb(q, gk_slot, wuk_t):
    n = q.shape[0]
    return pl.pallas_call(
        _absorb_body,
        out_shape=jax.ShapeDtypeStruct((N_HEADS, n, KV_LORA), F32),
        name="absorb_q",
    )(q, gk_slot, wuk_t)


def _paged_body(pt_ref, wt_ref, a_ref, qr_ref, ckv_hbm, kr_hbm, acc_ref, m_ref, l_ref,
                xs, krs, lhs, s_scr, sem, *, n_chunks, ppc, page_base):
    nk = ppc * PAGE_SIZE
    nope = N_HEADS * QK_NOPE
    b = pl.program_id(0)
    last = pl.num_programs(0) * n_chunks - 1

    def copies(t, slot):
        out = []
        for i in range(ppc):
            pg = pt_ref[t * ppc + i] + page_base
            keys = pl.ds(i * PAGE_SIZE, PAGE_SIZE)
            out.append(pltpu.make_async_copy(ckv_hbm.at[pg], xs.at[slot, keys, :], sem.at[slot, 0]))
            out.append(pltpu.make_async_copy(kr_hbm.at[pg], krs.at[slot, :, keys], sem.at[slot, 1]))
        return out

    def start(t, slot):
        for cp in copies(t, slot):
            cp.start()

    def wait(t, slot):
        for cp in copies(t, slot):
            cp.wait()

    def scores(t, slot):
        e = t // n_chunks
        wait(t, slot)
        lhs[nope:, :] = jnp.concatenate([a_ref[e], jnp.zeros((8, KV_LORA), F32)], axis=0).astype(BF16)
        qr = jnp.concatenate([qr_ref[e], jnp.zeros((8, QK_ROPE), F32)], axis=0).astype(BF16)
        out = _dot_nt(lhs[...], xs[slot].astype(BF16))
        k_t = out[0:nope]
        n = jnp.sum((k_t * k_t).reshape(N_HEADS, QK_NOPE, nk), axis=1)
        rope = _dot(qr, krs[slot].astype(BF16))[0:N_HEADS]
        return out[nope:nope + N_HEADS] * lax.rsqrt(n * (1.0 / QK_NOPE) + EPS) + rope

    @pl.when(b == 0)
    def _():
        lhs[0:nope, :] = wt_ref[...]
        for t0 in range(PAGE_SLOTS - 1):
            start(t0, t0)
        s_scr[...] = scores(0, 0)

    def step(c, carry):
        m, l, acc, s = carry
        t = b * n_chunks + c
        s_next = scores(jnp.minimum(t + 1, last), (t + 1) % PAGE_SLOTS)
        m_new = jnp.maximum(m, jnp.max(s, axis=-1, keepdims=True))
        alpha = jnp.exp2(m - m_new)
        p = jnp.exp2(s - m_new)
        l = alpha * l + jnp.sum(p, axis=-1, keepdims=True)
        p16 = jnp.concatenate([p, jnp.zeros_like(p)], axis=0).astype(BF16)
        acc = alpha * acc + _dot(p16, xs[t % PAGE_SLOTS].astype(BF16))[0:N_HEADS]
        ahead = PAGE_SLOTS - 1
        start(jnp.minimum(t + ahead, last), (t + ahead) % PAGE_SLOTS)
        return m_new, l, acc, s_next

    init = (jnp.full((N_HEADS, 1), NEG, F32), jnp.zeros((N_HEADS, 1), F32),
            jnp.zeros((N_HEADS, KV_LORA), F32), s_scr[...])
    m, l, acc, s = lax.fori_loop(0, n_chunks, step, init)
    s_scr[...] = s
    acc_ref[0] = acc
    m_ref[0] = jnp.broadcast_to(m, (N_HEADS, HEAD_SLOT))
    l_ref[0] = jnp.broadcast_to(l, (N_HEADS, HEAD_SLOT))

    @pl.when(b == pl.num_programs(0) - 1)
    def _():
        for extra in range(2, PAGE_SLOTS):
            wait(last, (last + extra) % PAGE_SLOTS)


def _paged_mla(page_table, wuk_t2d, a, qr, cache_ckv, cache_krope, page_base):
    nb, n_pages = page_table.shape
    ppc = min(PAGES_PER_CHUNK, n_pages)
    n_chunks = n_pages // ppc
    assert n_chunks * ppc == n_pages and nb * n_chunks >= PAGE_SLOTS
    nk = ppc * PAGE_SIZE
    vmem = pl.BlockSpec(memory_space=pltpu.VMEM)
    per_b = pl.BlockSpec((1, N_HEADS, HEAD_SLOT), lambda bi: (bi, 0, 0))
    out = jax.ShapeDtypeStruct((nb, N_HEADS, HEAD_SLOT), F32)
    return pl.pallas_call(
        functools.partial(_paged_body, n_chunks=n_chunks, ppc=ppc, page_base=page_base),
        grid=(nb,),
        in_specs=[pl.BlockSpec(memory_space=pltpu.SMEM), vmem, vmem, vmem,
                  pl.BlockSpec(memory_space=pl.ANY), pl.BlockSpec(memory_space=pl.ANY)],
        out_specs=(per_b, per_b, per_b),
        out_shape=(out, out, out),
        scratch_shapes=[pltpu.VMEM((PAGE_SLOTS, nk, KV_LORA), F32),
                        pltpu.VMEM((PAGE_SLOTS, QK_ROPE, nk), F32),
                        pltpu.VMEM((N_HEADS * QK_NOPE + 16, KV_LORA), BF16),
                        pltpu.VMEM((N_HEADS, nk), F32),
                        pltpu.SemaphoreType.DMA((PAGE_SLOTS, 2))],
        compiler_params=pltpu.CompilerParams(dimension_semantics=("arbitrary",)),
        name="paged_mla",
    )(page_table.reshape(-1), wuk_t2d, a, qr, cache_ckv, cache_krope)


def _smem_body(mq_ref, kt_ref, vt_ref, o_ref):
    bc, w, n_mem = kt_ref.shape
    step = pl.program_id(0)
    lane = lax.broadcasted_iota(jnp.int32, o_ref.shape, 1)

    @pl.when(step == 0)
    def _():
        o_ref[...] = jnp.zeros_like(o_ref)

    out = o_ref[...]
    for i in range(bc):
        q = jnp.concatenate([mq_ref[i]] * (n_mem // HEAD_SLOT), axis=1)
        s = jnp.sum((kt_ref[i] * q).reshape(MEM_HEADS, MEM_HEAD_DIM, n_mem), axis=1)
        p = jnp.exp2(s - jnp.max(s, axis=-1, keepdims=True))
        p = p / jnp.sum(p, axis=-1, keepdims=True)
        col = jnp.concatenate(
            [jnp.sum(vt_ref[i, h * MEM_HEAD_DIM:(h + 1) * MEM_HEAD_DIM, :] * p[h:h + 1, :],
                     axis=-1, keepdims=True) for h in range(MEM_HEADS)], axis=0)
        out = jnp.where(lane == step * bc + i, col, out)
    o_ref[...] = out


def _sample_mem_attend(mq_lanes, kt, vt, bc=8):
    nb, w, n_mem = kt.shape
    return pl.pallas_call(
        _smem_body,
        grid=(nb // bc,),
        in_specs=[pl.BlockSpec((bc, w, HEAD_SLOT), lambda i: (i, 0, 0)),
                  pl.BlockSpec((bc, w, n_mem), lambda i: (i, 0, 0)),
                  pl.BlockSpec((bc, w, n_mem), lambda i: (i, 0, 0))],
        out_specs=pl.BlockSpec((w, nb), lambda i: (0, 0)),
        out_shape=jax.ShapeDtypeStruct((w, nb), F32),
        compiler_params=pltpu.CompilerParams(dimension_semantics=("arbitrary",)),
        name="sample_mem_attend",
    )(mq_lanes, kt, vt)


def _spost_body(x_ref, q_ref, k_ref, ckv_ref, acc_ref, m_ref, l_ref, wuv_ref, u_ref, gb_ref,
                s0_ref, s1_ref, cw_ref, omem_ref, g1_ref, g2_ref, g3_ref, wo_ref, o_ref):
    ckv = ckv_ref[...]
    o_mla = jnp.zeros((x_ref.shape[0], MLA_WIDTH), F32)
    for h in range(N_HEADS):
        sl = slice(h * HEAD_SLOT, (h + 1) * HEAD_SLOT)
        s_new = jnp.sum(q_ref[:, sl].astype(F32) * k_ref[:, sl].astype(F32), axis=-1, keepdims=True)
        m_old = m_ref[h][:, 0:1]
        l_old = l_ref[h][:, 0:1]
        m_new = jnp.maximum(m_old, s_new)
        alpha = jnp.exp2(m_old - m_new)
        p_new = jnp.exp2(s_new - m_new)
        o_lat = (acc_ref[h] * alpha + p_new * ckv) / (l_old * alpha + p_new)
        o_mla += _dot(o_lat.astype(BF16), wuv_ref[h])
    cw = cw_ref[...]
    y = gb_ref[...] * (cw[0:1] * s0_ref[...] + cw[1:2] * s1_ref[...] + cw[2:3] * u_ref[...])
    o_ref[...] = _out_proj(x_ref[...], o_mla, y, omem_ref[...], g1_ref, g2_ref, g3_ref, wo_ref)


def _sample_postmix(*args):
    n = args[0].shape[0]
    return pl.pallas_call(
        _spost_body,
        out_shape=jax.ShapeDtypeStruct((n, D_MODEL), F32),
        name="sample_postmix",
    )(*args)


def _rope_tables(pos):
    inv_freq = ROPE_THETA ** (-jnp.arange(0, QK_ROPE, 2, dtype=F32) / QK_ROPE)
    ang = pos.astype(F32)[:, None] * inv_freq[None, :]
    cos, sin = jnp.cos(ang), jnp.sin(ang)
    n = pos.shape[0]
    zeros = lambda w_: jnp.zeros((n, w_), F32)
    tail = HEAD_SLOT - ROPE_LO - QK_ROPE
    c = jnp.concatenate([jnp.ones((n, ROPE_LO), F32), cos, cos, zeros(tail)], axis=1)
    s1 = jnp.concatenate([zeros(ROPE_LO), -sin, zeros(HALF_ROPE), zeros(tail)], axis=1)
    s2 = jnp.concatenate([zeros(ROPE_LO), zeros(HALF_ROPE), sin, zeros(tail)], axis=1)
    return c, s1, s2


def _block_diag_avg(sizes, width):
    idx = jnp.arange(width)
    gid = jnp.full((width,), -1, jnp.int32)
    scale = jnp.zeros((width,), F32)
    lo = 0
    for g, sz in enumerate(sizes):
        inside = (idx >= lo) & (idx < lo + sz)
        gid = jnp.where(inside, g, gid)
        scale = jnp.where(inside, 1.0 / sz, scale)
        lo += sz
    same = (gid[:, None] == gid[None, :]) & (gid[:, None] >= 0)
    return jnp.where(same, scale[None, :], 0.0).astype(BF16)


def _layer_weights(l, p):
    pad_last = lambda a, n: jnp.pad(a, [(0, 0)] * (a.ndim - 1) + [(0, n - a.shape[-1])])
    row = lambda v: v.reshape(1, -1).astype(F32)
    w_in = p["w_in"][l]
    zc = lambda n: jnp.zeros((D_MODEL, n), F32)
    w_in_p = jnp.concatenate([w_in[:, 0:384], zc(ROPE_LO), w_in[:, 384:416],
                              zc(HEAD_SLOT - ROPE_LO - QK_ROPE), w_in[:, 416:]], axis=1)
    slot_gain = lambda nope, rope: jnp.concatenate(
        [nope, rope, jnp.zeros((HEAD_SLOT - QK_NOPE - QK_ROPE,), F32)])
    zeros_n = jnp.zeros((QK_NOPE,), F32)
    zeros_r = jnp.zeros((QK_ROPE,), F32)
    g_kn = p["g_kn"][l]
    w_uk = p["w_uk"][l]
    g_slot = _block_diag_avg((QK_NOPE, QK_ROPE), HEAD_SLOT).astype(F32)
    g2 = jnp.kron(jnp.eye(2, dtype=F32), g_slot).astype(BF16)
    wuv = p["w_uv"][l]
    wuv_bd = jnp.stack([jnp.pad(wuv[:, h, :], ((0, 0), (h * V_HEAD, MLA_WIDTH - (h + 1) * V_HEAD)))
                        for h in range(N_HEADS)])
    return {
        "g_ffn1": row(p["g_ffn1"][l]), "w1_gate": p["w1_gate"][l].astype(BF16),
        "w1_up": p["w1_up"][l].astype(BF16), "w1_down": p["w1_down"][l].astype(BF16),
        "g_ffn2": row(p["g_ffn2"][l]), "w2_gate": p["w2_gate"][l].astype(BF16),
        "w2_up": p["w2_up"][l].astype(BF16), "w2_down": p["w2_down"][l].astype(BF16),
        "g_mix": row(p["g_mix"][l]), "w_in": w_in_p.astype(BF16),
        "g_q_lora": row(p["g_q_lora"][l]),
        "w_uq": pad_last(p["w_uq"][l].reshape(Q_LORA, N_HEADS, QK_NOPE + QK_ROPE), HEAD_SLOT)
        .reshape(Q_LORA, N_HEADS * HEAD_SLOT).astype(BF16),
        "gq_vec": row(jnp.tile(slot_gain(p["g_qn"][l], p["g_qr"][l]), N_HEADS) * (MLA_SCALE * LOG2E)),
        "G2": g2,
        "g_kv_lora": row(p["g_kv_lora"][l]),
        "gkr_vec": row(slot_gain(zeros_n, p["g_kr"][l])),
        "w_uk": pad_last(w_uk, HEAD_SLOT).reshape(KV_LORA, N_HEADS * HEAD_SLOT).astype(BF16),
        "gk_vec": row(jnp.tile(slot_gain(g_kn, zeros_r), N_HEADS)),
        "gk_slot": row(slot_gain(g_kn, zeros_r)),
        "w_uv": wuv.reshape(KV_LORA, MLA_WIDTH).astype(BF16),
        "wuv_bd": wuv_bd.astype(BF16),
        "wuk_t_pad": jnp.pad(jnp.transpose(w_uk, (1, 2, 0)), ((0, 0), (0, HEAD_SLOT - QK_NOPE), (0, 0)))
        .astype(BF16),
        "wuk_t2d": jnp.transpose(w_uk, (1, 2, 0)).reshape(N_HEADS * QK_NOPE, KV_LORA).astype(BF16),
        "gmq_vec": row(jnp.tile(p["g_mq"][l], MEM_HEADS) * (MEM_SCALE * LOG2E)),
        "G64": _block_diag_avg((MEM_HEAD_DIM,) * MEM_HEADS, MEM_WIDTH),
        "g_mem": row(p["g_mem"][l]), "w_mem_k": p["w_mem_k"][l].astype(BF16),
        "w_mem_v": p["w_mem_v"][l].astype(BF16),
        "gmk_vec": row(jnp.tile(p["g_mk"][l], MEM_HEADS)),
        "conv_w": p["conv_w"][l].astype(F32),
        "g_out_mla": row(p["g_out_mla"][l]), "g_out_conv": row(p["g_out_conv"][l]),
        "g_out_mem": row(p["g_out_mem"][l]), "w_o": p["w_o"][l].astype(BF16),
    }


def _mem_block_diag(mk, mv):
    b, n, _ = mk.shape
    k4 = mk.reshape(b, n, MEM_HEADS, MEM_HEAD_DIM)
    v4 = mv.reshape(b, n, MEM_HEADS, MEM_HEAD_DIM)
    eye = jnp.eye(MEM_HEADS, dtype=F32)
    k_bd = jnp.einsum("bnhd,hg->bhdgn", k4, eye).reshape(b, MEM_WIDTH, MEM_HEADS * n)
    v_bd = jnp.einsum("bnhd,hg->bhngd", v4, eye).reshape(b, MEM_HEADS * n, MEM_WIDTH)
    return k_bd.astype(BF16), v_bd.astype(BF16)


def _prompt_layer(x, mem, w, tables, tm):
    b, s, d = x.shape
    x = _ffn_half(x.reshape(b * s, d), w["g_ffn1"], w["w1_gate"], w["w1_up"], w["w1_down"],
                  tm).reshape(b, s, d)
    qt, k, vt, ckv, kr, u, gb, mq = _premix(x, tables, w, tm)
    o_mla = _mla_prompt(qt, k, vt, min(QUERY_TILE, s))
    mk, mv = _mem_kv(mem, w)
    mk_bd, mv_bd = _mem_block_diag(mk, mv)
    x = _postmix(x, o_mla, u, gb, mq, mk_bd, mv_bd, w, tm)
    x = _ffn_half(x.reshape(b * s, d), w["g_ffn2"], w["w2_gate"], w["w2_up"], w["w2_down"],
                  tm).reshape(b, s, d)
    return x, ckv, jnp.swapaxes(kr, 1, 2), u, mk, mv


def _sample_layer(x, w, tables, cache_ckv, cache_krope, page_base, page_table, state, mem_k, mem_v):
    n = x.shape[0]
    x = _ffn_half(x, w["g_ffn1"], w["w1_gate"], w["w1_up"], w["w1_down"], n)
    qt, k, _, ckv, kr, u, gb, mq = _premix(x[None], tables, w, n)
    q = qt[0].T
    a = _absorb(q, w["gk_slot"], w["wuk_t_pad"])
    qr = q.reshape(n, N_HEADS, HEAD_SLOT)[:, :, ROPE_LO:ROPE_LO + QK_ROPE].astype(F32)
    acc, m, l = _paged_mla(page_table, w["wuk_t2d"], jnp.transpose(a, (1, 0, 2)), qr,
                           cache_ckv, cache_krope, page_base)
    head_major = lambda t: jnp.transpose(t, (1, 0, 2))
    feature_major = lambda t: jnp.transpose(t, (0, 2, 3, 1)).reshape(n, MEM_WIDTH, -1)
    mq_lanes = jnp.broadcast_to(mq[0].astype(F32)[:, :, None], (n, MEM_WIDTH, HEAD_SLOT))
    o_mem = _sample_mem_attend(mq_lanes, feature_major(mem_k), feature_major(mem_v)).T
    x = _sample_postmix(x, q, k[0], ckv[0], head_major(acc), head_major(m), head_major(l),
                        w["wuv_bd"], u[0], gb[0], state[:, 0, :], state[:, 1, :], w["conv_w"], o_mem,
                        w["g_out_mla"], w["g_out_conv"], w["g_out_mem"], w["w_o"])
    x = _ffn_half(x, w["g_ffn2"], w["w2_gate"], w["w2_up"], w["w2_down"], n)
    return x, ckv[0], kr[0].T, jnp.stack([state[:, 1, :], u[0]], axis=1)


def kernel(x_prompt, mem_prompt, x_sample, cache_ckv, cache_krope, page_table, state_conv, cache_mem_k,
           cache_mem_v, g_ffn1, w1_gate, w1_up, w1_down, g_mix, w_in, g_q_lora, w_uq, g_qn, g_qr,
           g_kv_lora, w_uk, w_uv, g_kn, g_kr, conv_w, g_mem, w_mem_k, w_mem_v, g_mq, g_mk, g_out_mla,
           g_out_conv, g_out_mem, w_o, g_ffn2, w2_gate, w2_up, w2_down):
    params = dict(g_ffn1=g_ffn1, w1_gate=w1_gate, w1_up=w1_up, w1_down=w1_down, g_mix=g_mix, w_in=w_in,
                  g_q_lora=g_q_lora, w_uq=w_uq, g_qn=g_qn, g_qr=g_qr, g_kv_lora=g_kv_lora, w_uk=w_uk,
                  w_uv=w_uv, g_kn=g_kn, g_kr=g_kr, conv_w=conv_w, g_mem=g_mem, w_mem_k=w_mem_k,
                  w_mem_v=w_mem_v, g_mq=g_mq, g_mk=g_mk, g_out_mla=g_out_mla, g_out_conv=g_out_conv,
                  g_out_mem=g_out_mem, w_o=w_o, g_ffn2=g_ffn2, w2_gate=w2_gate, w2_up=w2_up,
                  w2_down=w2_down)
    depth = w_in.shape[0]
    b, s, _ = x_prompt.shape
    nb, dec_seq, _ = x_sample.shape
    assert dec_seq == 1
    n_phys = cache_ckv.shape[1]
    tm = min(ROW_TILE, s)
    tab_p = _rope_tables(jnp.arange(s))
    tab_s = _rope_tables(jnp.full((nb,), PAST_LEN, jnp.int32))
    ckv_pages = cache_ckv.reshape(depth * n_phys, PAGE_SIZE, KV_LORA)
    kr_pages = jnp.swapaxes(cache_krope, 2, 3).reshape(depth * n_phys, QK_ROPE, PAGE_SIZE)

    xp, xs = x_prompt, x_sample.reshape(nb, D_MODEL)
    outs_p, outs_s = [], []
    for l in range(depth):
        w = _layer_weights(l, params)
        xp, ckv, kr, u, mk, mv = _prompt_layer(xp, mem_prompt, w, tab_p, tm)
        outs_p.append((ckv, kr, u[:, -(CONV_K - 1):], mk.reshape(b, -1, MEM_HEADS, MEM_HEAD_DIM),
                       mv.reshape(b, -1, MEM_HEADS, MEM_HEAD_DIM)))
        xs, ckv_s, kr_s, conv_s = _sample_layer(xs, w, tab_s, ckv_pages, kr_pages, l * n_phys, page_table,
                                                state_conv[l], cache_mem_k[l], cache_mem_v[l])
        outs_s.append((ckv_s[:, None, :], kr_s[:, None, :], conv_s))
    stack = lambda items, i: jnp.stack([it[i] for it in items])
    return (xp, xs.reshape(nb, 1, D_MODEL), stack(outs_p, 0), stack(outs_p, 1), stack(outs_p, 2),
            stack(outs_p, 3), stack(outs_p, 4), stack(outs_s, 0), stack(outs_s, 1), stack(outs_s, 2))
```

```python
import functools

import jax
import jax.numpy as jnp
from jax import lax
from jax.experimental import pallas as pl
from jax.experimental.pallas import tpu as pltpu

F32 = jnp.float32
BF16 = jnp.bfloat16

D_MODEL = 1024
N_HEADS = 8
Q_LORA = 256
KV_LORA = 128
QK_NOPE = 64
QK_ROPE = 32
V_HEAD = 64
MLA_WIDTH = N_HEADS * V_HEAD
CONV_WIDTH = 256
CONV_K = 3
MEM_HEADS = 4
MEM_HEAD_DIM = 64
MEM_WIDTH = MEM_HEADS * MEM_HEAD_DIM
D_FF = 2816
ROPE_THETA = 10000.0
EPS = 1e-6
PAST_LEN = 16384
PAGE_SIZE = 128
MLA_SCALE = (QK_NOPE + QK_ROPE) ** -0.5
MEM_SCALE = MEM_HEAD_DIM ** -0.5
LOG2E = 1.4426950408889634

HEAD_SLOT = 128
ROPE_LO = QK_NOPE
HALF_ROPE = QK_ROPE // 2
Z_WIDTH = 1536
VT_ROWS = 80
NEG = -1e30
PAGES_PER_CHUNK = 64
PAGE_SLOTS = 4
ROW_TILE = 512
MIX_ROW_TILE = 1024
KEY_SUB = 256
QUERY_TILE = 1024
HEADS_PER_STEP = 4


def _rms(x, g):
    ms = jnp.mean(x * x, axis=-1, keepdims=True)
    return x * lax.rsqrt(ms + EPS) * g


def _dot(a, b):
    return jnp.dot(a, b, preferred_element_type=F32)


def _dot_nt(a, b):
    return lax.dot_general(a, b, (((1,), (1,)), ((), ())), preferred_element_type=F32)


def _group_mean_sq(x, g_ref):
    x2 = (x * x).astype(BF16)
    g = g_ref[...]
    cols = [_dot(x2[:, j * 256:(j + 1) * 256], g) for j in range(x.shape[1] // 256)]
    return cols[0] if len(cols) == 1 else jnp.concatenate(cols, axis=1)


def _rope_slab(v, cos, s1, s2):
    return (v * cos + pltpu.roll(v, HEAD_SLOT - HALF_ROPE, 1) * s1
            + pltpu.roll(v, HALF_ROPE, 1) * s2)


def _const_spec(shape):
    nd = len(shape)
    return pl.BlockSpec(shape, lambda *_: (0,) * nd, pipeline_mode=pl.Buffered(1))


def _swiglu_half_step(x, g_ref, wg_ref, wu_ref, wd_ref):
    xn = _rms(x, g_ref[...]).astype(BF16)
    h = _dot(xn, wg_ref[...])
    u = _dot(xn, wu_ref[...])
    a = (h / (1.0 + jnp.exp(-h)) * u).astype(BF16)
    return x + 0.5 * _dot(a, wd_ref[...])


def _ffn_body(x_ref, g_ref, wg_ref, wu_ref, wd_ref, o_ref):
    o_ref[...] = _swiglu_half_step(x_ref[...], g_ref, wg_ref, wu_ref, wd_ref)


def _ffn_half(x, g, wg, wu, wd, tm):
    n, d = x.shape
    dff = wg.shape[1]
    return pl.pallas_call(
        _ffn_body,
        grid=(n // tm,),
        in_specs=[pl.BlockSpec((tm, d), lambda i: (i, 0)),
                  _const_spec((1, d)), _const_spec((d, dff)), _const_spec((d, dff)),
                  _const_spec((dff, d))],
        out_specs=pl.BlockSpec((tm, d), lambda i: (i, 0)),
        out_shape=jax.ShapeDtypeStruct((n, d), F32),
        compiler_params=pltpu.CompilerParams(dimension_semantics=("parallel",)),
        name="ffn_half",
    )(x, g, wg, wu, wd)


def _premix_body(x_ref, cos_ref, s1_ref, s2_ref, gmix_ref, win_ref, gql_ref, wuq_ref, gq_ref,
                 g2_ref, gkvl_ref, gkr_ref, wuk_ref, gk_ref, wuv_ref, gmq_ref, g64_ref,
                 qt_ref, k_ref, vt_ref, ckv_ref, kr_ref, u_ref, gb_ref, mq_ref):
    tm = x_ref.shape[0]
    hn = _rms(x_ref[...], gmix_ref[...]).astype(BF16)
    z = _dot(hn, win_ref[...])
    c_q, c_kv, k_r = z[:, 0:256], z[:, 256:384], z[:, 384:512]
    u_in, g_b, g_c, m_q = z[:, 512:768], z[:, 768:1024], z[:, 1024:1280], z[:, 1280:1536]
    cos, s1, s2 = cos_ref[...], s1_ref[...], s2_ref[...]

    q = _dot(_rms(c_q, gql_ref[...]).astype(BF16), wuq_ref[...])
    q = q * lax.rsqrt(_group_mean_sq(q, g2_ref) + EPS) * gq_ref[...]
    q = jnp.concatenate(
        [_rope_slab(q[:, h * HEAD_SLOT:(h + 1) * HEAD_SLOT], cos, s1, s2) for h in range(N_HEADS)],
        axis=1)
    qt_ref[...] = q.T.astype(BF16)

    ckv = _rms(c_kv, gkvl_ref[...])
    ckv_ref[...] = ckv
    ckv16 = ckv.astype(BF16)
    kr_ms = jnp.sum(k_r * k_r, axis=-1, keepdims=True) * (1.0 / QK_ROPE)
    krr = _rope_slab(k_r * lax.rsqrt(kr_ms + EPS) * gkr_ref[...], cos, s1, s2)
    kr_ref[...] = krr.T[ROPE_LO:ROPE_LO + QK_ROPE, :]
    kk = _dot(ckv16, wuk_ref[...])
    kk = kk * lax.rsqrt(_group_mean_sq(kk, g2_ref) + EPS) * gk_ref[...]
    k_ref[...] = jnp.concatenate(
        [kk[:, h * HEAD_SLOT:(h + 1) * HEAD_SLOT] + krr for h in range(N_HEADS)],
        axis=1).astype(BF16)
    vt = _dot(ckv16, wuv_ref[...]).T
    tkv = vt_ref.shape[2]
    ones_rows = jnp.where(lax.broadcasted_iota(jnp.int32, (VT_ROWS - V_HEAD, tkv), 0) == 0,
                          1.0, 0.0).astype(BF16)
    for u in range(tm // tkv):
        for h in range(N_HEADS):
            vt_ref[u, h * VT_ROWS:h * VT_ROWS + V_HEAD, :] = (
                vt[h * V_HEAD:(h + 1) * V_HEAD, u * tkv:(u + 1) * tkv].astype(BF16))
            vt_ref[u, h * VT_ROWS + V_HEAD:(h + 1) * VT_ROWS, :] = ones_rows

    u_ref[...] = g_c * u_in
    gb_ref[...] = g_b
    mq = m_q * lax.rsqrt(_group_mean_sq(m_q, g64_ref) + EPS) * gmq_ref[...]
    mq_ref[...] = mq.astype(BF16)


def _premix(x, tables, w, tm):
    b, s, d = x.shape
    tkv = min(KEY_SUB, tm)
    cos, s1, s2 = tables
    row = lambda width: pl.BlockSpec((None, tm, width), lambda bi, i: (bi, i, 0))
    tab = pl.BlockSpec((tm, HEAD_SLOT), lambda bi, i: (i, 0))
    consts = [w["g_mix"], w["w_in"], w["g_q_lora"], w["w_uq"], w["gq_vec"], w["G2"], w["g_kv_lora"],
              w["gkr_vec"], w["w_uk"], w["gk_vec"], w["w_uv"], w["gmq_vec"], w["G64"]]
    out_shape = (
        jax.ShapeDtypeStruct((b, N_HEADS * HEAD_SLOT, s), BF16),
        jax.ShapeDtypeStruct((b, s, N_HEADS * HEAD_SLOT), BF16),
        jax.ShapeDtypeStruct((b, s // tkv, N_HEADS * VT_ROWS, tkv), BF16),
        jax.ShapeDtypeStruct((b, s, KV_LORA), F32),
        jax.ShapeDtypeStruct((b, QK_ROPE, s), F32),
        jax.ShapeDtypeStruct((b, s, CONV_WIDTH), F32),
        jax.ShapeDtypeStruct((b, s, CONV_WIDTH), F32),
        jax.ShapeDtypeStruct((b, s, MEM_WIDTH), BF16),
    )
    out_specs = (
        pl.BlockSpec((None, N_HEADS * HEAD_SLOT, tm), lambda bi, i: (bi, 0, i)),
        row(N_HEADS * HEAD_SLOT),
        pl.BlockSpec((None, tm // tkv, N_HEADS * VT_ROWS, tkv), lambda bi, i: (bi, i, 0, 0)),
        row(KV_LORA), pl.BlockSpec((None, QK_ROPE, tm), lambda bi, i: (bi, 0, i)),
        row(CONV_WIDTH), row(CONV_WIDTH), row(MEM_WIDTH),
    )
    return pl.pallas_call(
        _premix_body,
        grid=(b, s // tm),
        in_specs=[row(d), tab, tab, tab] + [_const_spec(c.shape) for c in consts],
        out_specs=out_specs,
        out_shape=out_shape,
        compiler_params=pltpu.CompilerParams(dimension_semantics=("parallel", "parallel")),
        name="premix",
    )(x, cos, s1, s2, *consts)


def _attn_body(qt_ref, k_ref, vt_ref, o_ref, s_even, s_odd, m_ref, acc_ref):
    tq = qt_ref.shape[1]
    tk = vt_ref.shape[2]
    nh = qt_ref.shape[0] // HEAD_SLOT
    n_diag = tq // tk
    assert n_diag * tk == tq and n_diag % 2 == 0
    qi = pl.program_id(2)
    q_t = [qt_ref[j * HEAD_SLOT:(j + 1) * HEAD_SLOT, :] for j in range(nh)]

    def scores(i, j, s_ref, lo=0):
        rows = pl.ds(pl.multiple_of(i * tk, tk), tk)
        s_ref[j, :, lo:] = _dot(k_ref[rows, j * HEAD_SLOT:(j + 1) * HEAD_SLOT], q_t[j][:, lo:])

    def absorb(i, j, s_ref, lo=0, diagonal=False):
        v = vt_ref[i, j * VT_ROWS:(j + 1) * VT_ROWS, :]
        for c in range(lo // tk, n_diag):
            cols = slice(c * tk, (c + 1) * tk)
            s_t = s_ref[j, :, cols]
            if diagonal and c * tk == lo:
                kpos = lax.broadcasted_iota(jnp.int32, s_t.shape, 0)
                qpos = lax.broadcasted_iota(jnp.int32, s_t.shape, 1)
                s_t = jnp.where(kpos <= qpos, s_t, NEG)
            m = m_ref[j, :, cols]
            m_new = jnp.maximum(m, jnp.max(s_t, axis=0, keepdims=True))
            p = jnp.exp2(s_t - m_new).astype(BF16)
            acc_ref[j, :, cols] = jnp.exp2(m - m_new) * acc_ref[j, :, cols] + _dot(v, p)
            m_ref[j, :, cols] = m_new

    def pair(g):
        scores(2 * g + 1, nh - 1, s_odd)
        for j in range(nh):
            absorb(2 * g, j, s_even)
            scores(2 * g + 2, j, s_even)
        for j in range(nh):
            absorb(2 * g + 1, j, s_odd)
            if j < nh - 1:
                scores(2 * g + 3, j, s_odd)

    def run_pairs(start, count, per_step):
        def body(g, carry):
            for u in range(per_step):
                pair(start + per_step * g + u)
            return carry
        lax.fori_loop(0, count // per_step, body, 0)
        return start + per_step * (count // per_step), count % per_step

    for j in range(nh):
        scores(0, j, s_even)
    for j in range(nh - 1):
        scores(1, j, s_odd)
    m_ref[...] = jnp.full(m_ref.shape, NEG, F32)
    acc_ref[...] = jnp.zeros(acc_ref.shape, F32)
    nxt, left = run_pairs(0, qi * (n_diag // 2), 4)
    nxt, left = run_pairs(nxt, left, 2)
    if n_diag % 4:
        run_pairs(nxt, left, 1)
    first = qi * n_diag
    scores(first + 1, nh - 1, s_odd, tk)
    for d in range(n_diag):
        buf = s_odd if d % 2 else s_even
        for j in range(nh):
            absorb(first + d, j, buf, d * tk, diagonal=True)
            if d + 2 < n_diag:
                scores(first + d + 2, j, buf, (d + 2) * tk)
    outs = [acc_ref[j, 0:V_HEAD, :] / acc_ref[j, V_HEAD:V_HEAD + 1, :] for j in range(nh)]
    o_ref[...] = jnp.concatenate(outs, axis=0).T


def _mla_prompt(qt, k, vt, tq):
    b, _, s = qt.shape
    nkt, tk = vt.shape[1], vt.shape[3]
    nh = HEADS_PER_STEP
    return pl.pallas_call(
        _attn_body,
        grid=(b, N_HEADS // nh, s // tq),
        in_specs=[pl.BlockSpec((None, nh * HEAD_SLOT, tq), lambda bi, p, qi: (bi, p, qi)),
                  pl.BlockSpec((None, s, nh * HEAD_SLOT), lambda bi, p, qi: (bi, 0, p)),
                  pl.BlockSpec((None, nkt, nh * VT_ROWS, tk), lambda bi, p, qi: (bi, 0, p, 0))],
        out_specs=pl.BlockSpec((None, tq, nh * V_HEAD), lambda bi, p, qi: (bi, qi, p)),
        out_shape=jax.ShapeDtypeStruct((b, s, MLA_WIDTH), F32),
        scratch_shapes=[pltpu.VMEM((nh, tk, tq), F32), pltpu.VMEM((nh, tk, tq), F32),
                        pltpu.VMEM((nh, 1, tq), F32), pltpu.VMEM((nh, VT_ROWS, tq), F32)],
        compiler_params=pltpu.CompilerParams(
            dimension_semantics=("parallel", "parallel", "arbitrary")),
        name="mla_prompt",
    )(qt, k, vt)


def _memkv_body(mem_ref, g_ref, wk_ref, wv_ref, g64_ref, gmk_ref, k_ref, v_ref):
    hm = _rms(mem_ref[...], g_ref[...]).astype(BF16)
    k = _dot(hm, wk_ref[...])
    k_ref[...] = k * lax.rsqrt(_group_mean_sq(k, g64_ref) + EPS) * gmk_ref[...]
    v_ref[...] = _dot(hm, wv_ref[...])


def _mem_kv(mem, w):
    b, n, d = mem.shape
    consts = [w["g_mem"], w["w_mem_k"], w["w_mem_v"], w["G64"], w["gmk_vec"]]
    blk = pl.BlockSpec((None, n, MEM_WIDTH), lambda bi: (bi, 0, 0))
    return pl.pallas_call(
        _memkv_body,
        grid=(b,),
        in_specs=[pl.BlockSpec((None, n, d), lambda bi: (bi, 0, 0))]
        + [_const_spec(c.shape) for c in consts],
        out_specs=(blk, blk),
        out_shape=(jax.ShapeDtypeStruct((b, n, MEM_WIDTH), F32),) * 2,
        compiler_params=pltpu.CompilerParams(dimension_semantics=("parallel",)),
        name="mem_kv",
    )(mem, *consts)


def _out_proj(x, o_mla, y_conv, o_mem, g1_ref, g2_ref, g3_ref, wo_ref):
    o = _dot(_rms(o_mla, g1_ref[...]).astype(BF16), wo_ref[0:MLA_WIDTH, :])
    o += _dot(_rms(y_conv, g2_ref[...]).astype(BF16), wo_ref[MLA_WIDTH:MLA_WIDTH + CONV_WIDTH, :])
    o += _dot(_rms(o_mem, g3_ref[...]).astype(BF16), wo_ref[MLA_WIDTH + CONV_WIDTH:, :])
    return x + o


def _postmix_body(x_ref, omla_ref, u_ref, uprev_ref, gb_ref, mq_ref, mk_ref, mv_ref, cw_ref,
                  g1_ref, g2_ref, g3_ref, wo_ref, o_ref, ubuf):
    tm = x_ref.shape[0]
    n_mem = mv_ref.shape[0] // MEM_HEADS
    u = u_ref[...]
    ubuf[0:8, :] = jnp.where(pl.program_id(1) == 0, 0.0, uprev_ref[...])
    ubuf[8:8 + tm, :] = u
    cw = cw_ref[...]
    y = cw[0:1] * ubuf[pl.ds(6, tm), :] + cw[1:2] * ubuf[pl.ds(7, tm), :] + cw[2:3] * u
    y = gb_ref[...] * y
    s = _dot(mq_ref[...], mk_ref[...])
    ps = []
    for h in range(MEM_HEADS):
        sh = s[:, h * n_mem:(h + 1) * n_mem]
        p = jnp.exp2(sh - jnp.max(sh, axis=-1, keepdims=True))
        ps.append((p / jnp.sum(p, axis=-1, keepdims=True)).astype(BF16))
    o_mem = _dot(jnp.concatenate(ps, axis=1), mv_ref[...])
    o_ref[...] = _out_proj(x_ref[...], omla_ref[...], y, o_mem, g1_ref, g2_ref, g3_ref, wo_ref)


def _postmix(x, o_mla, u, gb, mq, mk_bd, mv_bd, w, tm):
    b, s, d = x.shape
    row = lambda width: pl.BlockSpec((None, tm, width), lambda bi, i: (bi, i, 0))
    prev = pl.BlockSpec((None, 8, CONV_WIDTH), lambda bi, i: (bi, jnp.maximum(i * (tm // 8) - 1, 0), 0))
    per_b = lambda a: pl.BlockSpec((None,) + a.shape[1:], lambda bi, i: (bi, 0, 0))
    consts = [w["conv_w"], w["g_out_mla"], w["g_out_conv"], w["g_out_mem"], w["w_o"]]
    return pl.pallas_call(
        _postmix_body,
        grid=(b, s // tm),
        in_specs=[row(d), row(MLA_WIDTH), row(CONV_WIDTH), prev, row(CONV_WIDTH), row(MEM_WIDTH),
                  per_b(mk_bd), per_b(mv_bd)] + [_const_spec(c.shape) for c in consts],
        out_specs=row(d),
        out_shape=jax.ShapeDtypeStruct((b, s, d), F32),
        scratch_shapes=[pltpu.VMEM((tm + 8, CONV_WIDTH), F32)],
        compiler_params=pltpu.CompilerParams(dimension_semantics=("parallel", "parallel")),
        name="postmix",
    )(x, o_mla, u, u, gb, mq, mk_bd, mv_bd, *consts)


def _absorb_body(q_ref, gk_ref, wt_ref, a_ref):
    for h in range(N_HEADS):
        qh = q_ref[:, h * HEAD_SLOT:(h + 1) * HEAD_SLOT].astype(F32) * gk_ref[...]
        a_ref[h] = _dot(qh.astype(BF16), wt_ref[h])


def _absorb(q, gk_slot, wuk_t):
    n = q.shape[0]
    return pl.pallas_call(
        _absorb_body,
        out_shape=jax.ShapeDtypeStruct((N_HEADS, n, KV_LORA), F32),
        name="absorb_q",
    )(q, gk_slot, wuk_t)


def _paged_body(pt_ref, wt_ref, a_ref, qr_ref, ckv_hbm, kr_hbm, acc_ref, m_ref, l_ref,
                xs, krs, lhs, s_scr, sem, *, n_chunks, ppc, page_base):
    nk = ppc * PAGE_SIZE
    nope = N_HEADS * QK_NOPE
    b = pl.program_id(0)
    last = pl.num_programs(0) * n_chunks - 1

    def copies(t, slot):
        out = []
        for i in range(ppc):
            pg = pt_ref[t * ppc + i] + page_base
            keys = pl.ds(i * PAGE_SIZE, PAGE_SIZE)
            out.append(pltpu.make_async_copy(ckv_hbm.at[pg], xs.at[slot, keys, :], sem.at[slot, 0]))
            out.append(pltpu.make_async_copy(kr_hbm.at[pg], krs.at[slot, :, keys], sem.at[slot, 1]))
        return out

    def start(t, slot):
        for cp in copies(t, slot):
            cp.start()

    def wait(t, slot):
        for cp in copies(t, slot):
            cp.wait()

    def scores(t, slot):
        e = t // n_chunks
        wait(t, slot)
        lhs[nope:, :] = jnp.concatenate([a_ref[e], jnp.zeros((8, KV_LORA), F32)], axis=0).astype(BF16)
        qr = jnp.concatenate([qr_ref[e], jnp.zeros((8, QK_ROPE), F32)], axis=0).astype(BF16)
        out = _dot_nt(lhs[...], xs[slot].astype(BF16))
        k_t = out[0:nope]
        n = jnp.sum((k_t * k_t).reshape(N_HEADS, QK_NOPE, nk), axis=1)
        rope = _dot(qr, krs[slot].astype(BF16))[0:N_HEADS]
        return out[nope:nope + N_HEADS] * lax.rsqrt(n * (1.0 / QK_NOPE) + EPS) + rope

    @pl.when(b == 0)
    def _():
        lhs[0:nope, :] = wt_ref[...]
        for t0 in range(PAGE_SLOTS - 1):
            start(t0, t0)
        s_scr[...] = scores(0, 0)

    def step(c, carry):
        m, l, acc, s = carry
        t = b * n_chunks + c
        s_next = scores(jnp.minimum(t + 1, last), (t + 1) % PAGE_SLOTS)
        m_new = jnp.maximum(m, jnp.max(s, axis=-1, keepdims=True))
        alpha = jnp.exp2(m - m_new)
        p = jnp.exp2(s - m_new)
        l = alpha * l + jnp.sum(p, axis=-1, keepdims=True)
        p16 = jnp.concatenate([p, jnp.zeros_like(p)], axis=0).astype(BF16)
        acc = alpha * acc + _dot(p16, xs[t % PAGE_SLOTS].astype(BF16))[0:N_HEADS]
        ahead = PAGE_SLOTS - 1
        start(jnp.minimum(t + ahead, last), (t + ahead) % PAGE_SLOTS)
        return m_new, l, acc, s_next

    init = (jnp.full((N_HEADS, 1), NEG, F32), jnp.zeros((N_HEADS, 1), F32),
            jnp.zeros((N_HEADS, KV_LORA), F32), s_scr[...])
    m, l, acc, s = lax.fori_loop(0, n_chunks, step, init)
    s_scr[...] = s
    acc_ref[0] = acc
    m_ref[0] = jnp.broadcast_to(m, (N_HEADS, HEAD_SLOT))
    l_ref[0] = jnp.broadcast_to(l, (N_HEADS, HEAD_SLOT))

    @pl.when(b == pl.num_programs(0) - 1)
    def _():
        for extra in range(2, PAGE_SLOTS):
            wait(last, (last + extra) % PAGE_SLOTS)


def _paged_mla(page_table, wuk_t2d, a, qr, cache_ckv, cache_krope, page_base):
    nb, n_pages = page_table.shape
    ppc = min(PAGES_PER_CHUNK, n_pages)
    n_chunks = n_pages // ppc
    assert n_chunks * ppc == n_pages and nb * n_chunks >= PAGE_SLOTS
    nk = ppc * PAGE_SIZE
    vmem = pl.BlockSpec(memory_space=pltpu.VMEM)
    per_b = pl.BlockSpec((1, N_HEADS, HEAD_SLOT), lambda bi: (bi, 0, 0))
    out = jax.ShapeDtypeStruct((nb, N_HEADS, HEAD_SLOT), F32)
    return pl.pallas_call(
        functools.partial(_paged_body, n_chunks=n_chunks, ppc=ppc, page_base=page_base),
        grid=(nb,),
        in_specs=[pl.BlockSpec(memory_space=pltpu.SMEM), vmem, vmem, vmem,
                  pl.BlockSpec(memory_space=pl.ANY), pl.BlockSpec(memory_space=pl.ANY)],
        out_specs=(per_b, per_b, per_b),
        out_shape=(out, out, out),
        scratch_shapes=[pltpu.VMEM((PAGE_SLOTS, nk, KV_LORA), F32),
                        pltpu.VMEM((PAGE_SLOTS, QK_ROPE, nk), F32),
                        pltpu.VMEM((N_HEADS * QK_NOPE + 16, KV_LORA), BF16),
                        pltpu.VMEM((N_HEADS, nk), F32),
                        pltpu.SemaphoreType.DMA((PAGE_SLOTS, 2))],
        compiler_params=pltpu.CompilerParams(dimension_semantics=("arbitrary",)),
        name="paged_mla",
    )(page_table.reshape(-1), wuk_t2d, a, qr, cache_ckv, cache_krope)


def _smem_body(mq_ref, kt_ref, vt_ref, o_ref):
    bc, w, n_mem = kt_ref.shape
    step = pl.program_id(0)
    lane = lax.broadcasted_iota(jnp.int32, o_ref.shape, 1)

    @pl.when(step == 0)
    def _():
        o_ref[...] = jnp.zeros_like(o_ref)

    out = o_ref[...]
    for i in range(bc):
        q = jnp.concatenate([mq_ref[i]] * (n_mem // HEAD_SLOT), axis=1)
        s = jnp.sum((kt_ref[i] * q).reshape(MEM_HEADS, MEM_HEAD_DIM, n_mem), axis=1)
        p = jnp.exp2(s - jnp.max(s, axis=-1, keepdims=True))
        p = p / jnp.sum(p, axis=-1, keepdims=True)
        col = jnp.concatenate(
            [jnp.sum(vt_ref[i, h * MEM_HEAD_DIM:(h + 1) * MEM_HEAD_DIM, :] * p[h:h + 1, :],
                     axis=-1, keepdims=True) for h in range(MEM_HEADS)], axis=0)
        out = jnp.where(lane == step * bc + i, col, out)
    o_ref[...] = out


def _sample_mem_attend(mq_lanes, kt, vt, bc=8):
    nb, w, n_mem = kt.shape
    return pl.pallas_call(
        _smem_body,
        grid=(nb // bc,),
        in_specs=[pl.BlockSpec((bc, w, HEAD_SLOT), lambda i: (i, 0, 0)),
                  pl.BlockSpec((bc, w, n_mem), lambda i: (i, 0, 0)),
                  pl.BlockSpec((bc, w, n_mem), lambda i: (i, 0, 0))],
        out_specs=pl.BlockSpec((w, nb), lambda i: (0, 0)),
        out_shape=jax.ShapeDtypeStruct((w, nb), F32),
        compiler_params=pltpu.CompilerParams(dimension_semantics=("arbitrary",)),
        name="sample_mem_attend",
    )(mq_lanes, kt, vt)


def _spost_body(x_ref, q_ref, k_ref, ckv_ref, acc_ref, m_ref, l_ref, wuv_ref, u_ref, gb_ref,
                s0_ref, s1_ref, cw_ref, omem_ref, g1_ref, g2_ref, g3_ref, wo_ref, o_ref):
    ckv = ckv_ref[...]
    o_mla = jnp.zeros((x_ref.shape[0], MLA_WIDTH), F32)
    for h in range(N_HEADS):
        sl = slice(h * HEAD_SLOT, (h + 1) * HEAD_SLOT)
        s_new = jnp.sum(q_ref[:, sl].astype(F32) * k_ref[:, sl].astype(F32), axis=-1, keepdims=True)
        m_old = m_ref[h][:, 0:1]
        l_old = l_ref[h][:, 0:1]
        m_new = jnp.maximum(m_old, s_new)
        alpha = jnp.exp2(m_old - m_new)
        p_new = jnp.exp2(s_new - m_new)
        o_lat = (acc_ref[h] * alpha + p_new * ckv) / (l_old * alpha + p_new)
        o_mla += _dot(o_lat.astype(BF16), wuv_ref[h])
    cw = cw_ref[...]
    y = gb_ref[...] * (cw[0:1] * s0_ref[...] + cw[1:2] * s1_ref[...] + cw[2:3] * u_ref[...])
    o_ref[...] = _out_proj(x_ref[...], o_mla, y, omem_ref[...], g1_ref, g2_ref, g3_ref, wo_ref)


def _sample_postmix(*args):
    n = args[0].shape[0]
    return pl.pallas_call(
        _spost_body,
        out_shape=jax.ShapeDtypeStruct((n, D_MODEL), F32),
        name="sample_postmix",
    )(*args)


def _rope_tables(pos):
    inv_freq = ROPE_THETA ** (-jnp.arange(0, QK_ROPE, 2, dtype=F32) / QK_ROPE)
    ang = pos.astype(F32)[:, None] * inv_freq[None, :]
    cos, sin = jnp.cos(ang), jnp.sin(ang)
    n = pos.shape[0]
    zeros = lambda w_: jnp.zeros((n, w_), F32)
    tail = HEAD_SLOT - ROPE_LO - QK_ROPE
    c = jnp.concatenate([jnp.ones((n, ROPE_LO), F32), cos, cos, zeros(tail)], axis=1)
    s1 = jnp.concatenate([zeros(ROPE_LO), -sin, zeros(HALF_ROPE), zeros(tail)], axis=1)
    s2 = jnp.concatenate([zeros(ROPE_LO), zeros(HALF_ROPE), sin, zeros(tail)], axis=1)
    return c, s1, s2


def _block_diag_avg(sizes, width):
    idx = jnp.arange(width)
    gid = jnp.full((width,), -1, jnp.int32)
    scale = jnp.zeros((width,), F32)
    lo = 0
    for g, sz in enumerate(sizes):
        inside = (idx >= lo) & (idx < lo + sz)
        gid = jnp.where(inside, g, gid)
        scale = jnp.where(inside, 1.0 / sz, scale)
        lo += sz
    same = (gid[:, None] == gid[None, :]) & (gid[:, None] >= 0)
    return jnp.where(same, scale[None, :], 0.0).astype(BF16)


def _layer_weights(l, p):
    pad_last = lambda a, n: jnp.pad(a, [(0, 0)] * (a.ndim - 1) + [(0, n - a.shape[-1])])
    row = lambda v: v.reshape(1, -1).astype(F32)
    w_in = p["w_in"][l]
    zc = lambda n: jnp.zeros((D_MODEL, n), F32)
    w_in_p = jnp.concatenate([w_in[:, 0:384], zc(ROPE_LO), w_in[:, 384:416],
                              zc(HEAD_SLOT - ROPE_LO - QK_ROPE), w_in[:, 416:]], axis=1)
    slot_gain = lambda nope, rope: jnp.concatenate(
        [nope, rope, jnp.zeros((HEAD_SLOT - QK_NOPE - QK_ROPE,), F32)])
    zeros_n = jnp.zeros((QK_NOPE,), F32)
    zeros_r = jnp.zeros((QK_ROPE,), F32)
    g_kn = p["g_kn"][l]
    w_uk = p["w_uk"][l]
    g_slot = _block_diag_avg((QK_NOPE, QK_ROPE), HEAD_SLOT).astype(F32)
    g2 = jnp.kron(jnp.eye(2, dtype=F32), g_slot).astype(BF16)
    wuv = p["w_uv"][l]
    wuv_bd = jnp.stack([jnp.pad(wuv[:, h, :], ((0, 0), (h * V_HEAD, MLA_WIDTH - (h + 1) * V_HEAD)))
                        for h in range(N_HEADS)])
    return {
        "g_ffn1": row(p["g_ffn1"][l]), "w1_gate": p["w1_gate"][l].astype(BF16),
        "w1_up": p["w1_up"][l].astype(BF16), "w1_down": p["w1_down"][l].astype(BF16),
        "g_ffn2": row(p["g_ffn2"][l]), "w2_gate": p["w2_gate"][l].astype(BF16),
        "w2_up": p["w2_up"][l].astype(BF16), "w2_down": p["w2_down"][l].astype(BF16),
        "g_mix": row(p["g_mix"][l]), "w_in": w_in_p.astype(BF16),
        "g_q_lora": row(p["g_q_lora"][l]),
        "w_uq": pad_last(p["w_uq"][l].reshape(Q_LORA, N_HEADS, QK_NOPE + QK_ROPE), HEAD_SLOT)
        .reshape(Q_LORA, N_HEADS * HEAD_SLOT).astype(BF16),
        "gq_vec": row(jnp.tile(slot_gain(p["g_qn"][l], p["g_qr"][l]), N_HEADS) * (MLA_SCALE * LOG2E)),
        "G2": g2,
        "g_kv_lora": row(p["g_kv_lora"][l]),
        "gkr_vec": row(slot_gain(zeros_n, p["g_kr"][l])),
        "w_uk": pad_last(w_uk, HEAD_SLOT).reshape(KV_LORA, N_HEADS * HEAD_SLOT).astype(BF16),
        "gk_vec": row(jnp.tile(slot_gain(g_kn, zeros_r), N_HEADS)),
        "gk_slot": row(slot_gain(g_kn, zeros_r)),
        "w_uv": wuv.reshape(KV_LORA, MLA_WIDTH).astype(BF16),
        "wuv_bd": wuv_bd.astype(BF16),
        "wuk_t_pad": jnp.pad(jnp.transpose(w_uk, (1, 2, 0)), ((0, 0), (0, HEAD_SLOT - QK_NOPE), (0, 0)))
        .astype(BF16),
        "wuk_t2d": jnp.transpose(w_uk, (1, 2, 0)).reshape(N_HEADS * QK_NOPE, KV_LORA).astype(BF16),
        "gmq_vec": row(jnp.tile(p["g_mq"][l], MEM_HEADS) * (MEM_SCALE * LOG2E)),
        "G64": _block_diag_avg((MEM_HEAD_DIM,) * MEM_HEADS, MEM_WIDTH),
        "g_mem": row(p["g_mem"][l]), "w_mem_k": p["w_mem_k"][l].astype(BF16),
        "w_mem_v": p["w_mem_v"][l].astype(BF16),
        "gmk_vec": row(jnp.tile(p["g_mk"][l], MEM_HEADS)),
        "conv_w": p["conv_w"][l].astype(F32),
        "g_out_mla": row(p["g_out_mla"][l]), "g_out_conv": row(p["g_out_conv"][l]),
        "g_out_mem": row(p["g_out_mem"][l]), "w_o": p["w_o"][l].astype(BF16),
    }


def _mem_block_diag(mk, mv):
    b, n, _ = mk.shape
    k4 = mk.reshape(b, n, MEM_HEADS, MEM_HEAD_DIM)
    v4 = mv.reshape(b, n, MEM_HEADS, MEM_HEAD_DIM)
    eye = jnp.eye(MEM_HEADS, dtype=F32)
    k_bd = jnp.einsum("bnhd,hg->bhdgn", k4, eye).reshape(b, MEM_WIDTH, MEM_HEADS * n)
    v_bd = jnp.einsum("bnhd,hg->bhngd", v4, eye).reshape(b, MEM_HEADS * n, MEM_WIDTH)
    return k_bd.astype(BF16), v_bd.astype(BF16)


def _prompt_layer(x, mem, w, tables, tm):
    b, s, d = x.shape
    x = _ffn_half(x.reshape(b * s, d), w["g_ffn1"], w["w1_gate"], w["w1_up"], w["w1_down"],
                  tm).reshape(b, s, d)
    tmix = min(MIX_ROW_TILE, s)
    qt, k, vt, ckv, kr, u, gb, mq = _premix(x, tables, w, tmix)
    o_mla = _mla_prompt(qt, k, vt, min(QUERY_TILE, s))
    mk, mv = _mem_kv(mem, w)
    mk_bd, mv_bd = _mem_block_diag(mk, mv)
    x = _postmix(x, o_mla, u, gb, mq, mk_bd, mv_bd, w, tmix)
    x = _ffn_half(x.reshape(b * s, d), w["g_ffn2"], w["w2_gate"], w["w2_up"], w["w2_down"],
                  tm).reshape(b, s, d)
    return x, ckv, jnp.swapaxes(kr, 1, 2), u, mk, mv


def _sample_layer(x, w, tables, cache_ckv, cache_krope, page_base, page_table, state, mem_k, mem_v):
    n = x.shape[0]
    x = _ffn_half(x, w["g_ffn1"], w["w1_gate"], w["w1_up"], w["w1_down"], n)
    qt, k, _, ckv, kr, u, gb, mq = _premix(x[None], tables, w, n)
    q = qt[0].T
    a = _absorb(q, w["gk_slot"], w["wuk_t_pad"])
    qr = q.reshape(n, N_HEADS, HEAD_SLOT)[:, :, ROPE_LO:ROPE_LO + QK_ROPE].astype(F32)
    acc, m, l = _paged_mla(page_table, w["wuk_t2d"], jnp.transpose(a, (1, 0, 2)), qr,
                           cache_ckv, cache_krope, page_base)
    head_major = lambda t: jnp.transpose(t, (1, 0, 2))
    feature_major = lambda t: jnp.transpose(t, (0, 2, 3, 1)).reshape(n, MEM_WIDTH, -1)
    mq_lanes = jnp.broadcast_to(mq[0].astype(F32)[:, :, None], (n, MEM_WIDTH, HEAD_SLOT))
    o_mem = _sample_mem_attend(mq_lanes, feature_major(mem_k), feature_major(mem_v)).T
    x = _sample_postmix(x, q, k[0], ckv[0], head_major(acc), head_major(m), head_major(l),
                        w["wuv_bd"], u[0], gb[0], state[:, 0, :], state[:, 1, :], w["conv_w"], o_mem,
                        w["g_out_mla"], w["g_out_conv"], w["g_out_mem"], w["w_o"])
    x = _ffn_half(x, w["g_ffn2"], w["w2_gate"], w["w2_up"], w["w2_down"], n)
    return x, ckv[0], kr[0].T, jnp.stack([state[:, 1, :], u[0]], axis=1)


def kernel(x_prompt, mem_prompt, x_sample, cache_ckv, cache_krope, page_table, state_conv, cache_mem_k,
           cache_mem_v, g_ffn1, w1_gate, w1_up, w1_down, g_mix, w_in, g_q_lora, w_uq, g_qn, g_qr,
           g_kv_lora, w_uk, w_uv, g_kn, g_kr, conv_w, g_mem, w_mem_k, w_mem_v, g_mq, g_mk, g_out_mla,
           g_out_conv, g_out_mem, w_o, g_ffn2, w2_gate, w2_up, w2_down):
    params = dict(g_ffn1=g_ffn1, w1_gate=w1_gate, w1_up=w1_up, w1_down=w1_down, g_mix=g_mix, w_in=w_in,
                  g_q_lora=g_q_lora, w_uq=w_uq, g_qn=g_qn, g_qr=g_qr, g_kv_lora=g_kv_lora, w_uk=w_uk,
                  w_uv=w_uv, g_kn=g_kn, g_kr=g_kr, conv_w=conv_w, g_mem=g_mem, w_mem_k=w_mem_k,
                  w_mem_v=w_mem_v, g_mq=g_mq, g_mk=g_mk, g_out_mla=g_out_mla, g_out_conv=g_out_conv,
                  g_out_mem=g_out_mem, w_o=w_o, g_ffn2=g_ffn2, w2_gate=w2_gate, w2_up=w2_up,
                  w2_down=w2_down)
    depth = w_in.shape[0]
    b, s, _ = x_prompt.shape
    nb, dec_seq, _ = x_sample.shape
    assert dec_seq == 1
    n_phys = cache_ckv.shape[1]
    tm = min(ROW_TILE, s)
    tab_p = _rope_tables(jnp.arange(s))
    tab_s = _rope_tables(jnp.full((nb,), PAST_LEN, jnp.int32))
    ckv_pages = cache_ckv.reshape(depth * n_phys, PAGE_SIZE, KV_LORA)
    kr_pages = jnp.swapaxes(cache_krope, 2, 3).reshape(depth * n_phys, QK_ROPE, PAGE_SIZE)

    xp, xs = x_prompt, x_sample.reshape(nb, D_MODEL)
    outs_p, outs_s = [], []
    for l in range(depth):
        w = _layer_weights(l, params)
        xp, ckv, kr, u, mk, mv = _prompt_layer(xp, mem_prompt, w, tab_p, tm)
        outs_p.append((ckv, kr, u[:, -(CONV_K - 1):], mk.reshape(b, -1, MEM_HEADS, MEM_HEAD_DIM),
                       mv.reshape(b, -1, MEM_HEADS, MEM_HEAD_DIM)))
        xs, ckv_s, kr_s, conv_s = _sample_layer(xs, w, tab_s, ckv_pages, kr_pages, l * n_phys, page_table,
                                                state_conv[l], cache_mem_k[l], cache_mem_v[l])
        outs_s.append((ckv_s[:, None, :], kr_s[:, None, :], conv_s))
    stack = lambda items, i: jnp.stack([it[i] for it in items])
    return (xp, xs.reshape(nb, 1, D_MODEL), stack(outs_p, 0), stack(outs_p, 1), stack(outs_p, 2),
            stack(outs_p, 3), stack(outs_p, 4), stack(outs_s, 0), stack(outs_s, 1), stack(outs_s, 2))
```

```python
import functools

import jax
import jax.numpy as jnp
from jax import lax
from jax.experimental import pallas as pl
from jax.experimental.pallas import tpu as pltpu

F32 = jnp.float32
BF16 = jnp.bfloat16

D_MODEL = 1024
N_HEADS = 8
Q_LORA = 256
KV_LORA = 128
QK_NOPE = 64
QK_ROPE = 32
V_HEAD = 64
MLA_WIDTH = N_HEADS * V_HEAD
CONV_WIDTH = 256
CONV_K = 3
MEM_HEADS = 4
MEM_HEAD_DIM = 64
MEM_WIDTH = MEM_HEADS * MEM_HEAD_DIM
D_FF = 2816
ROPE_THETA = 10000.0
EPS = 1e-6
PAST_LEN = 16384
PAGE_SIZE = 128
MLA_SCALE = (QK_NOPE + QK_ROPE) ** -0.5
MEM_SCALE = MEM_HEAD_DIM ** -0.5
LOG2E = 1.4426950408889634

HEAD_SLOT = 128
ROPE_LO = QK_NOPE
HALF_ROPE = QK_ROPE // 2
Z_WIDTH = 1536
VT_ROWS = 80
NEG = -1e30
PAGES_PER_CHUNK = 64
PAGE_SLOTS = 4
ROW_TILE = 512
MIX_ROW_TILE = 1024
KEY_SUB = 256
QUERY_TILE = 1024
HEADS_PER_STEP = 4


def _rms(x, g):
    ms = jnp.mean(x * x, axis=-1, keepdims=True)
    return x * lax.rsqrt(ms + EPS) * g


def _dot(a, b):
    return jnp.dot(a, b, preferred_element_type=F32)


def _dot_nt(a, b):
    return lax.dot_general(a, b, (((1,), (1,)), ((), ())), preferred_element_type=F32)


def _group_mean_sq(x, g_ref):
    x2 = (x * x).astype(BF16)
    g = g_ref[...]
    cols = [_dot(x2[:, j * 256:(j + 1) * 256], g) for j in range(x.shape[1] // 256)]
    return cols[0] if len(cols) == 1 else jnp.concatenate(cols, axis=1)


def _rope_slab(v, cos, s1, s2):
    return (v * cos + pltpu.roll(v, HEAD_SLOT - HALF_ROPE, 1) * s1
            + pltpu.roll(v, HALF_ROPE, 1) * s2)


def _const_spec(shape):
    nd = len(shape)
    return pl.BlockSpec(shape, lambda *_: (0,) * nd, pipeline_mode=pl.Buffered(1))


def _swiglu_half_step(x, g_ref, wg_ref, wu_ref, wd_ref):
    xn = _rms(x, g_ref[...]).astype(BF16)
    h = _dot(xn, wg_ref[...])
    u = _dot(xn, wu_ref[...])
    a = (h / (1.0 + jnp.exp(-h)) * u).astype(BF16)
    return x + 0.5 * _dot(a, wd_ref[...])


def _ffn_body(x_ref, g_ref, wg_ref, wu_ref, wd_ref, o_ref):
    o_ref[...] = _swiglu_half_step(x_ref[...], g_ref, wg_ref, wu_ref, wd_ref)


def _ffn_half(x, g, wg, wu, wd, tm):
    n, d = x.shape
    dff = wg.shape[1]
    return pl.pallas_call(
        _ffn_body,
        grid=(n // tm,),
        in_specs=[pl.BlockSpec((tm, d), lambda i: (i, 0)),
                  _const_spec((1, d)), _const_spec((d, dff)), _const_spec((d, dff)),
                  _const_spec((dff, d))],
        out_specs=pl.BlockSpec((tm, d), lambda i: (i, 0)),
        out_shape=jax.ShapeDtypeStruct((n, d), F32),
        compiler_params=pltpu.CompilerParams(dimension_semantics=("parallel",)),
        name="ffn_half",
    )(x, g, wg, wu, wd)


def _premix_body(x_ref, cos_ref, s1_ref, s2_ref, gmix_ref, win_ref, gql_ref, wuq_ref, gq_ref,
                 g2_ref, gkvl_ref, gkr_ref, wuk_ref, gk_ref, wuv_ref, gmq_ref, g64_ref,
                 qt_ref, k_ref, vt_ref, ckv_ref, kr_ref, u_ref, gb_ref, mq_ref):
    tm = x_ref.shape[0]
    hn = _rms(x_ref[...], gmix_ref[...]).astype(BF16)
    z = _dot(hn, win_ref[...])
    c_q, c_kv, k_r = z[:, 0:256], z[:, 256:384], z[:, 384:512]
    u_in, g_b, g_c, m_q = z[:, 512:768], z[:, 768:1024], z[:, 1024:1280], z[:, 1280:1536]
    cos, s1, s2 = cos_ref[...], s1_ref[...], s2_ref[...]

    q = _dot(_rms(c_q, gql_ref[...]).astype(BF16), wuq_ref[...])
    q = q * lax.rsqrt(_group_mean_sq(q, g2_ref) + EPS) * gq_ref[...]
    q = jnp.concatenate(
        [_rope_slab(q[:, h * HEAD_SLOT:(h + 1) * HEAD_SLOT], cos, s1, s2) for h in range(N_HEADS)],
        axis=1)
    qt_ref[...] = q.T.astype(BF16)

    ckv = _rms(c_kv, gkvl_ref[...])
    ckv_ref[...] = ckv
    ckv16 = ckv.astype(BF16)
    kr_ms = jnp.sum(k_r * k_r, axis=-1, keepdims=True) * (1.0 / QK_ROPE)
    krr = _rope_slab(k_r * lax.rsqrt(kr_ms + EPS) * gkr_ref[...], cos, s1, s2)
    kr_ref[...] = krr.T[ROPE_LO:ROPE_LO + QK_ROPE, :]
    kk = _dot(ckv16, wuk_ref[...])
    kk = kk * lax.rsqrt(_group_mean_sq(kk, g2_ref) + EPS) * gk_ref[...]
    k_ref[...] = jnp.concatenate(
        [kk[:, h * HEAD_SLOT:(h + 1) * HEAD_SLOT] + krr for h in range(N_HEADS)],
        axis=1).astype(BF16)
    vt = _dot(ckv16, wuv_ref[...]).T
    tkv = vt_ref.shape[2]
    ones_rows = jnp.where(lax.broadcasted_iota(jnp.int32, (VT_ROWS - V_HEAD, tkv), 0) == 0,
                          1.0, 0.0).astype(BF16)
    for u in range(tm // tkv):
        for h in range(N_HEADS):
            vt_ref[u, h * VT_ROWS:h * VT_ROWS + V_HEAD, :] = (
                vt[h * V_HEAD:(h + 1) * V_HEAD, u * tkv:(u + 1) * tkv].astype(BF16))
            vt_ref[u, h * VT_ROWS + V_HEAD:(h + 1) * VT_ROWS, :] = ones_rows

    u_ref[...] = g_c * u_in
    gb_ref[...] = g_b
    mq = m_q * lax.rsqrt(_group_mean_sq(m_q, g64_ref) + EPS) * gmq_ref[...]
    mq_ref[...] = mq.astype(BF16)


def _premix(x, tables, w, tm):
    b, s, d = x.shape
    tkv = min(KEY_SUB, tm)
    cos, s1, s2 = tables
    row = lambda width: pl.BlockSpec((None, tm, width), lambda bi, i: (bi, i, 0))
    tab = pl.BlockSpec((tm, HEAD_SLOT), lambda bi, i: (i, 0))
    consts = [w["g_mix"], w["w_in"], w["g_q_lora"], w["w_uq"], w["gq_vec"], w["G2"], w["g_kv_lora"],
              w["gkr_vec"], w["w_uk"], w["gk_vec"], w["w_uv"], w["gmq_vec"], w["G64"]]
    out_shape = (
        jax.ShapeDtypeStruct((b, N_HEADS * HEAD_SLOT, s), BF16),
        jax.ShapeDtypeStruct((b, s, N_HEADS * HEAD_SLOT), BF16),
        jax.ShapeDtypeStruct((b, s // tkv, N_HEADS * VT_ROWS, tkv), BF16),
        jax.ShapeDtypeStruct((b, s, KV_LORA), F32),
        jax.ShapeDtypeStruct((b, QK_ROPE, s), F32),
        jax.ShapeDtypeStruct((b, s, CONV_WIDTH), F32),
        jax.ShapeDtypeStruct((b, s, CONV_WIDTH), F32),
        jax.ShapeDtypeStruct((b, s, MEM_WIDTH), BF16),
    )
    out_specs = (
        pl.BlockSpec((None, N_HEADS * HEAD_SLOT, tm), lambda bi, i: (bi, 0, i)),
        row(N_HEADS * HEAD_SLOT),
        pl.BlockSpec((None, tm // tkv, N_HEADS * VT_ROWS, tkv), lambda bi, i: (bi, i, 0, 0)),
        row(KV_LORA), pl.BlockSpec((None, QK_ROPE, tm), lambda bi, i: (bi, 0, i)),
        row(CONV_WIDTH), row(CONV_WIDTH), row(MEM_WIDTH),
    )
    return pl.pallas_call(
        _premix_body,
        grid=(b, s // tm),
        in_specs=[row(d), tab, tab, tab] + [_const_spec(c.shape) for c in consts],
        out_specs=out_specs,
        out_shape=out_shape,
        compiler_params=pltpu.CompilerParams(dimension_semantics=("parallel", "parallel")),
        name="premix",
    )(x, cos, s1, s2, *consts)


def _attn_body(qt_ref, k_ref, vt_ref, o_ref, s_even, s_odd, m_ref, acc_ref):
    tq = qt_ref.shape[1]
    tk = vt_ref.shape[2]
    nh = qt_ref.shape[0] // HEAD_SLOT
    n_diag = tq // tk
    assert n_diag * tk == tq and n_diag % 2 == 0
    qi = pl.program_id(2)
    q_t = [qt_ref[j * HEAD_SLOT:(j + 1) * HEAD_SLOT, :] for j in range(nh)]

    def scores(i, j, s_ref, lo=0):
        rows = pl.ds(pl.multiple_of(i * tk, tk), tk)
        s_ref[j, :, lo:] = _dot(k_ref[rows, j * HEAD_SLOT:(j + 1) * HEAD_SLOT], q_t[j][:, lo:])

    def absorb(i, j, s_ref, lo=0, diagonal=False):
        v = vt_ref[i, j * VT_ROWS:(j + 1) * VT_ROWS, :]
        for c in range(lo // tk, n_diag):
            cols = slice(c * tk, (c + 1) * tk)
            s_t = s_ref[j, :, cols]
            if diagonal and c * tk == lo:
                kpos = lax.broadcasted_iota(jnp.int32, s_t.shape, 0)
                qpos = lax.broadcasted_iota(jnp.int32, s_t.shape, 1)
                s_t = jnp.where(kpos <= qpos, s_t, NEG)
            m = m_ref[j, :, cols]
            m_new = jnp.maximum(m, jnp.max(s_t, axis=0, keepdims=True))
            p = jnp.exp2(s_t - m_new).astype(BF16)
            acc_ref[j, :, cols] = jnp.exp2(m - m_new) * acc_ref[j, :, cols] + _dot(v, p)
            m_ref[j, :, cols] = m_new

    def pair(g):
        scores(2 * g + 1, nh - 1, s_odd)
        for j in range(nh):
            absorb(2 * g, j, s_even)
            scores(2 * g + 2, j, s_even)
        for j in range(nh):
            absorb(2 * g + 1, j, s_odd)
            if j < nh - 1:
                scores(2 * g + 3, j, s_odd)

    def run_pairs(start, count, per_step):
        def body(g, carry):
            for u in range(per_step):
                pair(start + per_step * g + u)
            return carry
        lax.fori_loop(0, count // per_step, body, 0)
        return start + per_step * (count // per_step), count % per_step

    for j in range(nh):
        scores(0, j, s_even)
    for j in range(nh - 1):
        scores(1, j, s_odd)
    m_ref[...] = jnp.full(m_ref.shape, NEG, F32)
    acc_ref[...] = jnp.zeros(acc_ref.shape, F32)
    nxt, left = run_pairs(0, qi * (n_diag // 2), 4)
    nxt, left = run_pairs(nxt, left, 2)
    if n_diag % 4:
        run_pairs(nxt, left, 1)
    first = qi * n_diag
    scores(first + 1, nh - 1, s_odd, tk)
    for d in range(n_diag):
        buf = s_odd if d % 2 else s_even
        for j in range(nh):
            absorb(first + d, j, buf, d * tk, diagonal=True)
            if d + 2 < n_diag:
                scores(first + d + 2, j, buf, (d + 2) * tk)
    outs = [acc_ref[j, 0:V_HEAD, :] / acc_ref[j, V_HEAD:V_HEAD + 1, :] for j in range(nh)]
    o_ref[...] = jnp.concatenate(outs, axis=0).T


def _mla_prompt(qt, k, vt, tq):
    b, _, s = qt.shape
    nkt, tk = vt.shape[1], vt.shape[3]
    nh = HEADS_PER_STEP
    return pl.pallas_call(
        _attn_body,
        grid=(b, N_HEADS // nh, s // tq),
        in_specs=[pl.BlockSpec((None, nh * HEAD_SLOT, tq), lambda bi, p, qi: (bi, p, qi)),
                  pl.BlockSpec((None, s, nh * HEAD_SLOT), lambda bi, p, qi: (bi, 0, p)),
                  pl.BlockSpec((None, nkt, nh * VT_ROWS, tk), lambda bi, p, qi: (bi, 0, p, 0))],
        out_specs=pl.BlockSpec((None, tq, nh * V_HEAD), lambda bi, p, qi: (bi, qi, p)),
        out_shape=jax.ShapeDtypeStruct((b, s, MLA_WIDTH), F32),
        scratch_shapes=[pltpu.VMEM((nh, tk, tq), F32), pltpu.VMEM((nh, tk, tq), F32),
                        pltpu.VMEM((nh, 1, tq), F32), pltpu.VMEM((nh, VT_ROWS, tq), F32)],
        compiler_params=pltpu.CompilerParams(
            dimension_semantics=("parallel", "parallel", "arbitrary")),
        name="mla_prompt",
    )(qt, k, vt)


def _memkv_body(mem_ref, g_ref, wk_ref, wv_ref, g64_ref, gmk_ref, k_ref, v_ref):
    hm = _rms(mem_ref[...], g_ref[...]).astype(BF16)
    k = _dot(hm, wk_ref[...])
    k_ref[...] = k * lax.rsqrt(_group_mean_sq(k, g64_ref) + EPS) * gmk_ref[...]
    v_ref[...] = _dot(hm, wv_ref[...])


def _mem_kv(mem, w):
    b, n, d = mem.shape
    consts = [w["g_mem"], w["w_mem_k"], w["w_mem_v"], w["G64"], w["gmk_vec"]]
    blk = pl.BlockSpec((None, n, MEM_WIDTH), lambda bi: (bi, 0, 0))
    return pl.pallas_call(
        _memkv_body,
        grid=(b,),
        in_specs=[pl.BlockSpec((None, n, d), lambda bi: (bi, 0, 0))]
        + [_const_spec(c.shape) for c in consts],
        out_specs=(blk, blk),
        out_shape=(jax.ShapeDtypeStruct((b, n, MEM_WIDTH), F32),) * 2,
        compiler_params=pltpu.CompilerParams(dimension_semantics=("parallel",)),
        name="mem_kv",
    )(mem, *consts)


def _out_proj(x, o_mla, y_conv, o_mem, g1_ref, g2_ref, g3_ref, wo_ref):
    o = _dot(_rms(o_mla, g1_ref[...]).astype(BF16), wo_ref[0:MLA_WIDTH, :])
    o += _dot(_rms(y_conv, g2_ref[...]).astype(BF16), wo_ref[MLA_WIDTH:MLA_WIDTH + CONV_WIDTH, :])
    o += _dot(_rms(o_mem, g3_ref[...]).astype(BF16), wo_ref[MLA_WIDTH + CONV_WIDTH:, :])
    return x + o


def _postmix_body(x_ref, omla_ref, u_ref, uprev_ref, gb_ref, mq_ref, mk_ref, mv_ref, cw_ref,
                  g1_ref, g2_ref, g3_ref, wo_ref, o_ref, ubuf):
    tm = x_ref.shape[0]
    n_mem = mv_ref.shape[0] // MEM_HEADS
    u = u_ref[...]
    ubuf[0:8, :] = jnp.where(pl.program_id(1) == 0, 0.0, uprev_ref[...])
    ubuf[8:8 + tm, :] = u
    cw = cw_ref[...]
    y = cw[0:1] * ubuf[pl.ds(6, tm), :] + cw[1:2] * ubuf[pl.ds(7, tm), :] + cw[2:3] * u
    y = gb_ref[...] * y
    s = _dot(mq_ref[...], mk_ref[...])
    ps = []
    for h in range(MEM_HEADS):
        sh = s[:, h * n_mem:(h + 1) * n_mem]
        p = jnp.exp2(sh - jnp.max(sh, axis=-1, keepdims=True))
        ps.append((p / jnp.sum(p, axis=-1, keepdims=True)).astype(BF16))
    o_mem = _dot(jnp.concatenate(ps, axis=1), mv_ref[...])
    o_ref[...] = _out_proj(x_ref[...], omla_ref[...], y, o_mem, g1_ref, g2_ref, g3_ref, wo_ref)


def _postmix(x, o_mla, u, gb, mq, mk_bd, mv_bd, w, tm):
    b, s, d = x.shape
    row = lambda width: pl.BlockSpec((None, tm, width), lambda bi, i: (bi, i, 0))
    prev = pl.BlockSpec((None, 8, CONV_WIDTH), lambda bi, i: (bi, jnp.maximum(i * (tm // 8) - 1, 0), 0))
    per_b = lambda a: pl.BlockSpec((None,) + a.shape[1:], lambda bi, i: (bi, 0, 0))
    consts = [w["conv_w"], w["g_out_mla"], w["g_out_conv"], w["g_out_mem"], w["w_o"]]
    return pl.pallas_call(
        _postmix_body,
        grid=(b, s // tm),
        in_specs=[row(d), row(MLA_WIDTH), row(CONV_WIDTH), prev, row(CONV_WIDTH), row(MEM_WIDTH),
                  per_b(mk_bd), per_b(mv_bd)] + [_const_spec(c.shape) for c in consts],
        out_specs=row(d),
        out_shape=jax.ShapeDtypeStruct((b, s, d), F32),
        scratch_shapes=[pltpu.VMEM((tm + 8, CONV_WIDTH), F32)],
        compiler_params=pltpu.CompilerParams(dimension_semantics=("parallel", "parallel")),
        name="postmix",
    )(x, o_mla, u, u, gb, mq, mk_bd, mv_bd, *consts)


def _absorb_body(q_ref, gk_ref, wt_ref, a_ref):
    for h in range(N_HEADS):
        qh = q_ref[:, h * HEAD_SLOT:(h + 1) * HEAD_SLOT].astype(F32) * gk_ref[...]
        a_ref[h] = _dot(qh.astype(BF16), wt_ref[h])


def _absorb(q, gk_slot, wuk_t):
    n = q.shape[0]
    return pl.pallas_call(
        _absorb_body,
        out_shape=jax.ShapeDtypeStruct((N_HEADS, n, KV_LORA), F32),
        name="absorb_q",
    )(q, gk_slot, wuk_t)


def _paged_body(pt_ref, wt_ref, a_ref, qr_ref, ckv_hbm, kr_hbm, acc_ref, m_ref, l_ref,
                xs, krs, lhs, s_scr, sem, *, n_chunks, ppc, page_base):
    nk = ppc * PAGE_SIZE
    nope = N_HEADS * QK_NOPE
    b = pl.program_id(0)
    last = pl.num_programs(0) * n_chunks - 1

    def copies(t, slot):
        out = []
        for i in range(ppc):
            pg = pt_ref[t * ppc + i] + page_base
            keys = pl.ds(i * PAGE_SIZE, PAGE_SIZE)
            out.append(pltpu.make_async_copy(ckv_hbm.at[pg], xs.at[slot, keys, :], sem.at[slot, 0]))
            out.append(pltpu.make_async_copy(kr_hbm.at[pg], krs.at[slot, :, keys], sem.at[slot, 1]))
        return out

    def start(t, slot):
        for idx, cp in enumerate(copies(t, slot)):
            cp.start(priority=idx % 2)

    def wait(t, slot):
        for cp in copies(t, slot):
            cp.wait()

    def scores(t, slot):
        e = t // n_chunks
        wait(t, slot)
        lhs[nope:, :] = jnp.concatenate([a_ref[e], jnp.zeros((8, KV_LORA), F32)], axis=0).astype(BF16)
        qr = jnp.concatenate([qr_ref[e], jnp.zeros((8, QK_ROPE), F32)], axis=0).astype(BF16)
        out = _dot_nt(lhs[...], xs[slot].astype(BF16))
        k_t = out[0:nope]
        n = jnp.sum((k_t * k_t).reshape(N_HEADS, QK_NOPE, nk), axis=1)
        rope = _dot(qr, krs[slot].astype(BF16))[0:N_HEADS]
        return out[nope:nope + N_HEADS] * lax.rsqrt(n * (1.0 / QK_NOPE) + EPS) + rope

    @pl.when(b == 0)
    def _():
        lhs[0:nope, :] = wt_ref[...]
        for t0 in range(PAGE_SLOTS - 1):
            start(t0, t0)
        s_scr[...] = scores(0, 0)

    def step(c, carry):
        m, l, acc, s = carry
        t = b * n_chunks + c
        s_next = scores(jnp.minimum(t + 1, last), (t + 1) % PAGE_SLOTS)
        m_new = jnp.maximum(m, jnp.max(s, axis=-1, keepdims=True))
        alpha = jnp.exp2(m - m_new)
        p = jnp.exp2(s - m_new)
        l = alpha * l + jnp.sum(p, axis=-1, keepdims=True)
        p16 = jnp.concatenate([p, jnp.zeros_like(p)], axis=0).astype(BF16)
        acc = alpha * acc + _dot(p16, xs[t % PAGE_SLOTS].astype(BF16))[0:N_HEADS]
        ahead = PAGE_SLOTS - 1
        start(jnp.minimum(t + ahead, last), (t + ahead) % PAGE_SLOTS)
        return m_new, l, acc, s_next

    init = (jnp.full((N_HEADS, 1), NEG, F32), jnp.zeros((N_HEADS, 1), F32),
            jnp.zeros((N_HEADS, KV_LORA), F32), s_scr[...])
    m, l, acc, s = lax.fori_loop(0, n_chunks, step, init)
    s_scr[...] = s
    acc_ref[0] = acc
    m_ref[0] = jnp.broadcast_to(m, (N_HEADS, HEAD_SLOT))
    l_ref[0] = jnp.broadcast_to(l, (N_HEADS, HEAD_SLOT))

    @pl.when(b == pl.num_programs(0) - 1)
    def _():
        for extra in range(2, PAGE_SLOTS):
            wait(last, (last + extra) % PAGE_SLOTS)


def _paged_mla(page_table, wuk_t2d, a, qr, cache_ckv, cache_krope, page_base):
    nb, n_pages = page_table.shape
    ppc = min(PAGES_PER_CHUNK, n_pages)
    n_chunks = n_pages // ppc
    assert n_chunks * ppc == n_pages and nb * n_chunks >= PAGE_SLOTS
    nk = ppc * PAGE_SIZE
    vmem = pl.BlockSpec(memory_space=pltpu.VMEM)
    per_b = pl.BlockSpec((1, N_HEADS, HEAD_SLOT), lambda bi: (bi, 0, 0))
    out = jax.ShapeDtypeStruct((nb, N_HEADS, HEAD_SLOT), F32)
    return pl.pallas_call(
        functools.partial(_paged_body, n_chunks=n_chunks, ppc=ppc, page_base=page_base),
        grid=(nb,),
        in_specs=[pl.BlockSpec(memory_space=pltpu.SMEM), vmem, vmem, vmem,
                  pl.BlockSpec(memory_space=pl.ANY), pl.BlockSpec(memory_space=pl.ANY)],
        out_specs=(per_b, per_b, per_b),
        out_shape=(out, out, out),
        scratch_shapes=[pltpu.VMEM((PAGE_SLOTS, nk, KV_LORA), F32),
                        pltpu.VMEM((PAGE_SLOTS, QK_ROPE, nk), F32),
                        pltpu.VMEM((N_HEADS * QK_NOPE + 16, KV_LORA), BF16),
                        pltpu.VMEM((N_HEADS, nk), F32),
                        pltpu.SemaphoreType.DMA((PAGE_SLOTS, 2))],
        compiler_params=pltpu.CompilerParams(dimension_semantics=("arbitrary",)),
        name="paged_mla",
    )(page_table.reshape(-1), wuk_t2d, a, qr, cache_ckv, cache_krope)


def _smem_body(mq_ref, kt_ref, vt_ref, o_ref):
    bc, w, n_mem = kt_ref.shape
    step = pl.program_id(0)
    lane = lax.broadcasted_iota(jnp.int32, o_ref.shape, 1)

    @pl.when(step == 0)
    def _():
        o_ref[...] = jnp.zeros_like(o_ref)

    out = o_ref[...]
    for i in range(bc):
        q = jnp.concatenate([mq_ref[i]] * (n_mem // HEAD_SLOT), axis=1)
        s = jnp.sum((kt_ref[i] * q).reshape(MEM_HEADS, MEM_HEAD_DIM, n_mem), axis=1)
        p = jnp.exp2(s - jnp.max(s, axis=-1, keepdims=True))
        p = p / jnp.sum(p, axis=-1, keepdims=True)
        col = jnp.concatenate(
            [jnp.sum(vt_ref[i, h * MEM_HEAD_DIM:(h + 1) * MEM_HEAD_DIM, :] * p[h:h + 1, :],
                     axis=-1, keepdims=True) for h in range(MEM_HEADS)], axis=0)
        out = jnp.where(lane == step * bc + i, col, out)
    o_ref[...] = out


def _sample_mem_attend(mq_lanes, kt, vt, bc=8):
    nb, w, n_mem = kt.shape
    return pl.pallas_call(
        _smem_body,
        grid=(nb // bc,),
        in_specs=[pl.BlockSpec((bc, w, HEAD_SLOT), lambda i: (i, 0, 0)),
                  pl.BlockSpec((bc, w, n_mem), lambda i: (i, 0, 0)),
                  pl.BlockSpec((bc, w, n_mem), lambda i: (i, 0, 0))],
        out_specs=pl.BlockSpec((w, nb), lambda i: (0, 0)),
        out_shape=jax.ShapeDtypeStruct((w, nb), F32),
        compiler_params=pltpu.CompilerParams(dimension_semantics=("arbitrary",)),
        name="sample_mem_attend",
    )(mq_lanes, kt, vt)


def _spost_body(x_ref, q_ref, k_ref, ckv_ref, acc_ref, m_ref, l_ref, wuv_ref, u_ref, gb_ref,
                s0_ref, s1_ref, cw_ref, omem_ref, g1_ref, g2_ref, g3_ref, wo_ref, o_ref):
    ckv = ckv_ref[...]
    o_mla = jnp.zeros((x_ref.shape[0], MLA_WIDTH), F32)
    for h in range(N_HEADS):
        sl = slice(h * HEAD_SLOT, (h + 1) * HEAD_SLOT)
        s_new = jnp.sum(q_ref[:, sl].astype(F32) * k_ref[:, sl].astype(F32), axis=-1, keepdims=True)
        m_old = m_ref[h][:, 0:1]
        l_old = l_ref[h][:, 0:1]
        m_new = jnp.maximum(m_old, s_new)
        alpha = jnp.exp2(m_old - m_new)
        p_new = jnp.exp2(s_new - m_new)
        o_lat = (acc_ref[h] * alpha + p_new * ckv) / (l_old * alpha + p_new)
        o_mla += _dot(o_lat.astype(BF16), wuv_ref[h])
    cw = cw_ref[...]
    y = gb_ref[...] * (cw[0:1] * s0_ref[...] + cw[1:2] * s1_ref[...] + cw[2:3] * u_ref[...])
    o_ref[...] = _out_proj(x_ref[...], o_mla, y, omem_ref[...], g1_ref, g2_ref, g3_ref, wo_ref)


def _sample_postmix(*args):
    n = args[0].shape[0]
    return pl.pallas_call(
        _spost_body,
        out_shape=jax.ShapeDtypeStruct((n, D_MODEL), F32),
        name="sample_postmix",
    )(*args)


def _rope_tables(pos):
    inv_freq = ROPE_THETA ** (-jnp.arange(0, QK_ROPE, 2, dtype=F32) / QK_ROPE)
    ang = pos.astype(F32)[:, None] * inv_freq[None, :]
    cos, sin = jnp.cos(ang), jnp.sin(ang)
    n = pos.shape[0]
    zeros = lambda w_: jnp.zeros((n, w_), F32)
    tail = HEAD_SLOT - ROPE_LO - QK_ROPE
    c = jnp.concatenate([jnp.ones((n, ROPE_LO), F32), cos, cos, zeros(tail)], axis=1)
    s1 = jnp.concatenate([zeros(ROPE_LO), -sin, zeros(HALF_ROPE), zeros(tail)], axis=1)
    s2 = jnp.concatenate([zeros(ROPE_LO), zeros(HALF_ROPE), sin, zeros(tail)], axis=1)
    return c, s1, s2


def _block_diag_avg(sizes, width):
    idx = jnp.arange(width)
    gid = jnp.full((width,), -1, jnp.int32)
    scale = jnp.zeros((width,), F32)
    lo = 0
    for g, sz in enumerate(sizes):
        inside = (idx >= lo) & (idx < lo + sz)
        gid = jnp.where(inside, g, gid)
        scale = jnp.where(inside, 1.0 / sz, scale)
        lo += sz
    same = (gid[:, None] == gid[None, :]) & (gid[:, None] >= 0)
    return jnp.where(same, scale[None, :], 0.0).astype(BF16)


def _layer_weights(l, p):
    pad_last = lambda a, n: jnp.pad(a, [(0, 0)] * (a.ndim - 1) + [(0, n - a.shape[-1])])
    row = lambda v: v.reshape(1, -1).astype(F32)
    w_in = p["w_in"][l]
    zc = lambda n: jnp.zeros((D_MODEL, n), F32)
    w_in_p = jnp.concatenate([w_in[:, 0:384], zc(ROPE_LO), w_in[:, 384:416],
                              zc(HEAD_SLOT - ROPE_LO - QK_ROPE), w_in[:, 416:]], axis=1)
    slot_gain = lambda nope, rope: jnp.concatenate(
        [nope, rope, jnp.zeros((HEAD_SLOT - QK_NOPE - QK_ROPE,), F32)])
    zeros_n = jnp.zeros((QK_NOPE,), F32)
    zeros_r = jnp.zeros((QK_ROPE,), F32)
    g_kn = p["g_kn"][l]
    w_uk = p["w_uk"][l]
    g_slot = _block_diag_avg((QK_NOPE, QK_ROPE), HEAD_SLOT).astype(F32)
    g2 = jnp.kron(jnp.eye(2, dtype=F32), g_slot).astype(BF16)
    wuv = p["w_uv"][l]
    wuv_bd = jnp.stack([jnp.pad(wuv[:, h, :], ((0, 0), (h * V_HEAD, MLA_WIDTH - (h + 1) * V_HEAD)))
                        for h in range(N_HEADS)])
    return {
        "g_ffn1": row(p["g_ffn1"][l]), "w1_gate": p["w1_gate"][l].astype(BF16),
        "w1_up": p["w1_up"][l].astype(BF16), "w1_down": p["w1_down"][l].astype(BF16),
        "g_ffn2": row(p["g_ffn2"][l]), "w2_gate": p["w2_gate"][l].astype(BF16),
        "w2_up": p["w2_up"][l].astype(BF16), "w2_down": p["w2_down"][l].astype(BF16),
        "g_mix": row(p["g_mix"][l]), "w_in": w_in_p.astype(BF16),
        "g_q_lora": row(p["g_q_lora"][l]),
        "w_uq": pad_last(p["w_uq"][l].reshape(Q_LORA, N_HEADS, QK_NOPE + QK_ROPE), HEAD_SLOT)
        .reshape(Q_LORA, N_HEADS * HEAD_SLOT).astype(BF16),
        "gq_vec": row(jnp.tile(slot_gain(p["g_qn"][l], p["g_qr"][l]), N_HEADS) * (MLA_SCALE * LOG2E)),
        "G2": g2,
        "g_kv_lora": row(p["g_kv_lora"][l]),
        "gkr_vec": row(slot_gain(zeros_n, p["g_kr"][l])),
        "w_uk": pad_last(w_uk, HEAD_SLOT).reshape(KV_LORA, N_HEADS * HEAD_SLOT).astype(BF16),
        "gk_vec": row(jnp.tile(slot_gain(g_kn, zeros_r), N_HEADS)),
        "gk_slot": row(slot_gain(g_kn, zeros_r)),
        "w_uv": wuv.reshape(KV_LORA, MLA_WIDTH).astype(BF16),
        "wuv_bd": wuv_bd.astype(BF16),
        "wuk_t_pad": jnp.pad(jnp.transpose(w_uk, (1, 2, 0)), ((0, 0), (0, HEAD_SLOT - QK_NOPE), (0, 0)))
        .astype(BF16),
        "wuk_t2d": jnp.transpose(w_uk, (1, 2, 0)).reshape(N_HEADS * QK_NOPE, KV_LORA).astype(BF16),
        "gmq_vec": row(jnp.tile(p["g_mq"][l], MEM_HEADS) * (MEM_SCALE * LOG2E)),
        "G64": _block_diag_avg((MEM_HEAD_DIM,) * MEM_HEADS, MEM_WIDTH),
        "g_mem": row(p["g_mem"][l]), "w_mem_k": p["w_mem_k"][l].astype(BF16),
        "w_mem_v": p["w_mem_v"][l].astype(BF16),
        "gmk_vec": row(jnp.tile(p["g_mk"][l], MEM_HEADS)),
        "conv_w": p["conv_w"][l].astype(F32),
        "g_out_mla": row(p["g_out_mla"][l]), "g_out_conv": row(p["g_out_conv"][l]),
        "g_out_mem": row(p["g_out_mem"][l]), "w_o": p["w_o"][l].astype(BF16),
    }


def _mem_block_diag(mk, mv):
    b, n, _ = mk.shape
    k4 = mk.reshape(b, n, MEM_HEADS, MEM_HEAD_DIM)
    v4 = mv.reshape(b, n, MEM_HEADS, MEM_HEAD_DIM)
    eye = jnp.eye(MEM_HEADS, dtype=F32)
    k_bd = jnp.einsum("bnhd,hg->bhdgn", k4, eye).reshape(b, MEM_WIDTH, MEM_HEADS * n)
    v_bd = jnp.einsum("bnhd,hg->bhngd", v4, eye).reshape(b, MEM_HEADS * n, MEM_WIDTH)
    return k_bd.astype(BF16), v_bd.astype(BF16)


def _prompt_layer(x, mem, w, tables, tm):
    b, s, d = x.shape
    x = _ffn_half(x.reshape(b * s, d), w["g_ffn1"], w["w1_gate"], w["w1_up"], w["w1_down"],
                  tm).reshape(b, s, d)
    tmix = min(MIX_ROW_TILE, s)
    qt, k, vt, ckv, kr, u, gb, mq = _premix(x, tables, w, tmix)
    o_mla = _mla_prompt(qt, k, vt, min(QUERY_TILE, s))
    mk, mv = _mem_kv(mem, w)
    mk_bd, mv_bd = _mem_block_diag(mk, mv)
    x = _postmix(x, o_mla, u, gb, mq, mk_bd, mv_bd, w, tmix)
    x = _ffn_half(x.reshape(b * s, d), w["g_ffn2"], w["w2_gate"], w["w2_up"], w["w2_down"],
                  tm).reshape(b, s, d)
    return x, ckv, jnp.swapaxes(kr, 1, 2), u, mk, mv


def _sample_layer(x, w, tables, cache_ckv, cache_krope, page_base, page_table, state, mem_k, mem_v):
    n = x.shape[0]
    x = _ffn_half(x, w["g_ffn1"], w["w1_gate"], w["w1_up"], w["w1_down"], n)
    qt, k, _, ckv, kr, u, gb, mq = _premix(x[None], tables, w, n)
    q = qt[0].T
    a = _absorb(q, w["gk_slot"], w["wuk_t_pad"])
    qr = q.reshape(n, N_HEADS, HEAD_SLOT)[:, :, ROPE_LO:ROPE_LO + QK_ROPE].astype(F32)
    acc, m, l = _paged_mla(page_table, w["wuk_t2d"], jnp.transpose(a, (1, 0, 2)), qr,
                           cache_ckv, cache_krope, page_base)
    head_major = lambda t: jnp.transpose(t, (1, 0, 2))
    feature_major = lambda t: jnp.transpose(t, (0, 2, 3, 1)).reshape(n, MEM_WIDTH, -1)
    mq_lanes = jnp.broadcast_to(mq[0].astype(F32)[:, :, None], (n, MEM_WIDTH, HEAD_SLOT))
    o_mem = _sample_mem_attend(mq_lanes, feature_major(mem_k), feature_major(mem_v)).T
    x = _sample_postmix(x, q, k[0], ckv[0], head_major(acc), head_major(m), head_major(l),
                        w["wuv_bd"], u[0], gb[0], state[:, 0, :], state[:, 1, :], w["conv_w"], o_mem,
                        w["g_out_mla"], w["g_out_conv"], w["g_out_mem"], w["w_o"])
    x = _ffn_half(x, w["g_ffn2"], w["w2_gate"], w["w2_up"], w["w2_down"], n)
    return x, ckv[0], kr[0].T, jnp.stack([state[:, 1, :], u[0]], axis=1)


def kernel(x_prompt, mem_prompt, x_sample, cache_ckv, cache_krope, page_table, state_conv, cache_mem_k,
           cache_mem_v, g_ffn1, w1_gate, w1_up, w1_down, g_mix, w_in, g_q_lora, w_uq, g_qn, g_qr,
           g_kv_lora, w_uk, w_uv, g_kn, g_kr, conv_w, g_mem, w_mem_k, w_mem_v, g_mq, g_mk, g_out_mla,
           g_out_conv, g_out_mem, w_o, g_ffn2, w2_gate, w2_up, w2_down):
    params = dict(g_ffn1=g_ffn1, w1_gate=w1_gate, w1_up=w1_up, w1_down=w1_down, g_mix=g_mix, w_in=w_in,
                  g_q_lora=g_q_lora, w_uq=w_uq, g_qn=g_qn, g_qr=g_qr, g_kv_lora=g_kv_lora, w_uk=w_uk,
                  w_uv=w_uv, g_kn=g_kn, g_kr=g_kr, conv_w=conv_w, g_mem=g_mem, w_mem_k=w_mem_k,
                  w_mem_v=w_mem_v, g_mq=g_mq, g_mk=g_mk, g_out_mla=g_out_mla, g_out_conv=g_out_conv,
                  g_out_mem=g_out_mem, w_o=w_o, g_ffn2=g_ffn2, w2_gate=w2_gate, w2_up=w2_up,
                  w2_down=w2_down)
    depth = w_in.shape[0]
    b, s, _ = x_prompt.shape
    nb, dec_seq, _ = x_sample.shape
    assert dec_seq == 1
    n_phys = cache_ckv.shape[1]
    tm = min(ROW_TILE, s)
    tab_p = _rope_tables(jnp.arange(s))
    tab_s = _rope_tables(jnp.full((nb,), PAST_LEN, jnp.int32))
    ckv_pages = cache_ckv.reshape(depth * n_phys, PAGE_SIZE, KV_LORA)
    kr_pages = jnp.swapaxes(cache_krope, 2, 3).reshape(depth * n_phys, QK_ROPE, PAGE_SIZE)

    xp, xs = x_prompt, x_sample.reshape(nb, D_MODEL)
    outs_p, outs_s = [], []
    for l in range(depth):
        w = _layer_weights(l, params)
        xp, ckv, kr, u, mk, mv = _prompt_layer(xp, mem_prompt, w, tab_p, tm)
        outs_p.append((ckv, kr, u[:, -(CONV_K - 1):], mk.reshape(b, -1, MEM_HEADS, MEM_HEAD_DIM),
                       mv.reshape(b, -1, MEM_HEADS, MEM_HEAD_DIM)))
        xs, ckv_s, kr_s, conv_s = _sample_layer(xs, w, tab_s, ckv_pages, kr_pages, l * n_phys, page_table,
                                                state_conv[l], cache_mem_k[l], cache_mem_v[l])
        outs_s.append((ckv_s[:, None, :], kr_s[:, None, :], conv_s))
    stack = lambda items, i: jnp.stack([it[i] for it in items])
    return (xp, xs.reshape(nb, 1, D_MODEL), stack(outs_p, 0), stack(outs_p, 1), stack(outs_p, 2),
            stack(outs_p, 3), stack(outs_p, 4), stack(outs_s, 0), stack(outs_s, 1), stack(outs_s, 2))
```
